```python
import math
import jax, jax.numpy as jnp
from jax import lax
import numpy as np

D_MODEL = 1024
BATCH = 16
SEQ = 4096
DEPTH = 4

N_MEM = 256
A_HEADS = 4
A_QK_DIM = 64
A_V_DIM = 2 * A_QK_DIM
A_QK_COLS = A_HEADS * 2 * A_QK_DIM
A_WIDTH = A_HEADS * A_V_DIM
POOL_WINDOWS = (2, 4, 8, 16)
POOL_GROUPS = len(POOL_WINDOWS)
POOL_GROUP_DIM = (D_MODEL // 2) // POOL_GROUPS
POOL_WIDTH = POOL_GROUPS * POOL_GROUP_DIM
EVEN_IN = 2 * A_QK_COLS + A_WIDTH + POOL_WIDTH
EVEN_MIX = A_WIDTH + POOL_WIDTH
CONV_WIDTH = 3
CONV_DIM = D_MODEL
X_HEADS = 4
X_HEAD_DIM = D_MODEL // X_HEADS
D_FF = 4 * D_MODEL
REL_BUCKETS = 32
REL_MAX_EXACT = REL_BUCKETS // 2
REL_MAX_DIST = 128
Q_BLOCK = 128
N_EVEN = (DEPTH + 1) // 2
N_ODD = DEPTH // 2
EPS = 1e-6

kernel_name = "hybrid_diffattn_pool_shortconv_trunk"


def rms_norm(x, g):
    xf = x.astype(jnp.float32)
    y = xf * lax.rsqrt(jnp.mean(xf * xf, axis=-1, keepdims=True) + EPS)
    return (y * g.astype(jnp.float32)).astype(x.dtype)


def t5_bucket(n):
    small = n < REL_MAX_EXACT
    nf = jnp.maximum(n, 1).astype(jnp.float32)
    large = REL_MAX_EXACT + (jnp.log(nf / REL_MAX_EXACT)
                             / math.log(REL_MAX_DIST / REL_MAX_EXACT)
                             * (REL_BUCKETS - REL_MAX_EXACT)).astype(jnp.int32)
    large = jnp.minimum(large, REL_BUCKETS - 1)
    return jnp.where(small, n, large)


def diff_attention(q1, q2, k1, k2, v, lam, bias_dist):
    B, S = q1.shape[0], q1.shape[1]
    nb = S // Q_BLOCK
    scale = A_QK_DIM ** -0.5
    qb1 = q1.reshape(B, nb, Q_BLOCK, A_HEADS, A_QK_DIM).transpose(1, 0, 2, 3, 4)
    qb2 = q2.reshape(B, nb, Q_BLOCK, A_HEADS, A_QK_DIM).transpose(1, 0, 2, 3, 4)
    starts = jnp.arange(nb, dtype=jnp.int32) * Q_BLOCK
    kpos = jnp.arange(S, dtype=jnp.int32)

    def block(args):
        qa, qc, start = args
        qpos = start + jnp.arange(Q_BLOCK, dtype=jnp.int32)
        dist = qpos[:, None] - kpos[None, :]
        causal = dist >= 0
        bias = bias_dist[:, jnp.clip(dist, 0, S - 1)]

        def probs(q, k):
            s = jnp.einsum('bqhd,bkhd->bhqk', q, k).astype(jnp.float32) * scale + bias
            s = jnp.where(causal, s, -jnp.inf)
            return jax.nn.softmax(s, axis=-1)

        a = probs(qa, k1) - lam * probs(qc, k2)
        return jnp.einsum('bhqk,bkhd->bqhd', a.astype(v.dtype), v)

    out = lax.map(block, (qb1, qb2, starts))
    return out.transpose(1, 0, 2, 3, 4).reshape(B, S, A_HEADS, A_V_DIM)


def multi_scale_pool(u, pool_w, pool_scale):
    B, S = u.shape[0], u.shape[1]
    ug = u.reshape(B, S, POOL_GROUPS, POOL_GROUP_DIM).astype(jnp.float32)
    c = jnp.concatenate([jnp.zeros((B, 1, POOL_GROUPS, POOL_GROUP_DIM), jnp.float32),
                         jnp.cumsum(ug, axis=1)], axis=1)
    t = jnp.arange(S, dtype=jnp.int32)
    pooled = []
    for gi, w in enumerate(POOL_WINDOWS):
        cg = c[:, :, gi]
        cp = jnp.concatenate([jnp.zeros((B, w - 1, POOL_GROUP_DIM), jnp.float32), cg], axis=1)
        win_sum = cg[:, 1:] - cp[:, :S]
        count = jnp.minimum(t + 1, w).astype(jnp.float32)[None, :, None]
        pooled.append(win_sum / count - ug[:, :, gi])
    p = jnp.stack(pooled, axis=2).astype(u.dtype)
    y = jnp.einsum('bsgc,gcd->bsgd', p, pool_w) * pool_scale.reshape(POOL_GROUPS, POOL_GROUP_DIM)
    return y.reshape(B, S, POOL_WIDTH)


def even_mixer(h, w_in, w_out, lq1, lk1, lq2, lk2, subln_g, pool_w, pool_scale,
               bias_dist, lambda_init):
    B, S = h.shape[0], h.shape[1]
    proj = h @ w_in
    q = proj[..., :A_QK_COLS].reshape(B, S, A_HEADS, 2, A_QK_DIM)
    k = proj[..., A_QK_COLS:2 * A_QK_COLS].reshape(B, S, A_HEADS, 2, A_QK_DIM)
    v = proj[..., 2 * A_QK_COLS:2 * A_QK_COLS + A_WIDTH].reshape(B, S, A_HEADS, A_V_DIM)
    u = proj[..., 2 * A_QK_COLS + A_WIDTH:]
    lam = (jnp.exp(jnp.sum(lq1.astype(jnp.float32) * lk1.astype(jnp.float32)))
           - jnp.exp(jnp.sum(lq2.astype(jnp.float32) * lk2.astype(jnp.float32)))
           + lambda_init)
    o = diff_attention(q[..., 0, :], q[..., 1, :], k[..., 0, :], k[..., 1, :], v, lam, bias_dist)
    o = (rms_norm(o, subln_g) * (1.0 - lambda_init)).reshape(B, S, A_WIDTH)
    y = multi_scale_pool(u, pool_w, pool_scale)
    return jnp.concatenate([o, y], axis=-1) @ w_out


def odd_mixer(h, w_in, conv_w, w_out):
    S = h.shape[1]
    proj = h @ w_in
    b_gate = proj[..., :CONV_DIM]
    c_gate = proj[..., CONV_DIM:2 * CONV_DIM]
    z = c_gate * proj[..., 2 * CONV_DIM:]
    zp = jnp.pad(z, ((0, 0), (CONV_WIDTH - 1, 0), (0, 0)))
    y = zp[:, 0:S] * conv_w[0]
    for tap in range(1, CONV_WIDTH):
        y = y + zp[:, tap:tap + S] * conv_w[tap]
    return (b_gate * y) @ w_out


def cross_attention(h, mem_n, wq, wkv, wo):
    B, S = h.shape[0], h.shape[1]
    M = mem_n.shape[1]
    q = (h @ wq).reshape(B, S, X_HEADS, X_HEAD_DIM)
    kv = (mem_n @ wkv).reshape(B, M, 2, X_HEADS, X_HEAD_DIM)
    s = jnp.einsum('bshd,bmhd->bhsm', q, kv[:, :, 0]).astype(jnp.float32) * X_HEAD_DIM ** -0.5
    p = jax.nn.softmax(s, axis=-1)
    o = jnp.einsum('bhsm,bmhd->bshd', p.astype(h.dtype), kv[:, :, 1]).reshape(B, S, D_MODEL)
    return o @ wo


def sq_relu_mlp(h, w1, w2):
    a = jax.nn.relu(h @ w1)
    return (a * a) @ w2


def setup_inputs(seed: int = 0) -> dict:
    key = jax.random.key(seed)
    ks = iter(jax.random.split(key, 32))
    f32 = jnp.float32

    def nrm(shape, scale):
        return jax.random.normal(next(ks), shape, f32) * scale

    def gain(shape):
        return 1.0 + nrm(shape, 0.02)

    return {
        "x": nrm((BATCH, SEQ, D_MODEL), 1.0),
        "mem": nrm((BATCH, N_MEM, D_MODEL), 1.0),
        "rel_bias": nrm((REL_BUCKETS, A_HEADS), 0.5),
        "mem_norm_g": gain((D_MODEL,)),
        "norm_mix_g": gain((DEPTH, D_MODEL)),
        "norm_xattn_g": gain((DEPTH, D_MODEL)),
        "norm_mlp_g": gain((DEPTH, D_MODEL)),
        "final_norm_g": gain((D_MODEL,)),
        "ab_w_in": nrm((N_EVEN, D_MODEL, EVEN_IN), D_MODEL ** -0.5),
        "ab_w_out": nrm((N_EVEN, EVEN_MIX, D_MODEL), EVEN_MIX ** -0.5),
        "lambda_q1": nrm((N_EVEN, A_QK_DIM), 0.1),
        "lambda_k1": nrm((N_EVEN, A_QK_DIM), 0.1),
        "lambda_q2": nrm((N_EVEN, A_QK_DIM), 0.1),
        "lambda_k2": nrm((N_EVEN, A_QK_DIM), 0.1),
        "subln_g": gain((N_EVEN, A_V_DIM)),
        "pool_w": nrm((N_EVEN, POOL_GROUPS, POOL_GROUP_DIM, POOL_GROUP_DIM), POOL_GROUP_DIM ** -0.5),
        "pool_scale": 1.0 + nrm((N_EVEN, POOL_WIDTH), 0.1),
        "conv_w_in": nrm((N_ODD, D_MODEL, 3 * CONV_DIM), D_MODEL ** -0.5),
        "conv_w": nrm((N_ODD, CONV_WIDTH, CONV_DIM), CONV_WIDTH ** -0.5),
        "conv_w_out": nrm((N_ODD, CONV_DIM, D_MODEL), CONV_DIM ** -0.5),
        "xattn_wq": nrm((DEPTH, D_MODEL, D_MODEL), D_MODEL ** -0.5),
        "xattn_wkv": nrm((DEPTH, D_MODEL, 2 * D_MODEL), D_MODEL ** -0.5),
        "xattn_wo": nrm((DEPTH, D_MODEL, D_MODEL), D_MODEL ** -0.5),
        "mlp_w1": nrm((DEPTH, D_MODEL, D_FF), D_MODEL ** -0.5),
        "mlp_w2": nrm((DEPTH, D_FF, D_MODEL), D_FF ** -0.5),
    }


def reference(x, mem, rel_bias, mem_norm_g, norm_mix_g, norm_xattn_g, norm_mlp_g,
              final_norm_g, ab_w_in, ab_w_out, lambda_q1, lambda_k1, lambda_q2,
              lambda_k2, subln_g, pool_w, pool_scale, conv_w_in, conv_w, conv_w_out,
              xattn_wq, xattn_wkv, xattn_wo, mlp_w1, mlp_w2):
    S = x.shape[1]
    buckets = t5_bucket(jnp.arange(S, dtype=jnp.int32))
    bias_dist = rel_bias.astype(jnp.float32)[buckets].T
    mem_n = rms_norm(mem, mem_norm_g)
    h = x
    for l in range(DEPTH):
        i = l // 2
        hn = rms_norm(h, norm_mix_g[l])
        if l % 2 == 0:
            lambda_init = 0.8 - 0.6 * math.exp(-0.3 * l)
            h = h + even_mixer(hn, ab_w_in[i], ab_w_out[i], lambda_q1[i], lambda_k1[i],
                               lambda_q2[i], lambda_k2[i], subln_g[i], pool_w[i],
                               pool_scale[i], bias_dist, lambda_init)
        else:
            h = h + odd_mixer(hn, conv_w_in[i], conv_w[i], conv_w_out[i])
        h = h + cross_attention(rms_norm(h, norm_xattn_g[l]), mem_n,
                                xattn_wq[l], xattn_wkv[l], xattn_wo[l])
        h = h + sq_relu_mlp(rms_norm(h, norm_mlp_g[l]), mlp_w1[l], mlp_w2[l])
    return rms_norm(h, final_norm_g)
```

```python
import functools
import math

import jax
import jax.numpy as jnp
from jax import lax
from jax.experimental import pallas as pl
from jax.experimental.pallas import tpu as pltpu

F32 = jnp.float32
BF16 = jnp.bfloat16

EPS = 1e-6
A_HEADS = 4
A_QK_DIM = 64
A_V_DIM = 2 * A_QK_DIM
A_WIDTH = A_HEADS * A_V_DIM
POOL_WINDOWS = (2, 4, 8, 16)
POOL_GROUP_DIM = 128
POOL_WIDTH = len(POOL_WINDOWS) * POOL_GROUP_DIM
POOL_HALO = 16
CONV_WIDTH = 3
CONV_HALO = 8
X_HEADS = 4
REL_BUCKETS = 32
REL_MAX_EXACT = REL_BUCKETS // 2
REL_MAX_DIST = 128

V7X_VMEM_LIMIT_CAP = 56 * 1024 * 1024

ROW_TILE = 512
ATTN_TILE = 512
NEG_BIG = -1e30


def _vmem_limit(pipelined_bytes, resident_bytes, live_bytes):
    return min(2 * pipelined_bytes + resident_bytes + live_bytes, V7X_VMEM_LIMIT_CAP)


def _params(semantics, vmem):
    return pltpu.CompilerParams(dimension_semantics=semantics, vmem_limit_bytes=vmem)


def _resident(shape):
    return pl.BlockSpec(shape, lambda *_: (0,) * len(shape), pipeline_mode=pl.Buffered(1))


def _rms(x, g):
    return x * lax.rsqrt(jnp.mean(x * x, axis=-1, keepdims=True) + EPS) * g


def _dot(a, b):
    return jnp.dot(a, b, preferred_element_type=F32)


def _dot_nt(a, b):
    return lax.dot_general(a, b, (((1,), (1,)), ((), ())), preferred_element_type=F32)


def _memkv_kernel(x_ref, g_ref, w_ref, o_ref):
    xn = _rms(x_ref[...], g_ref[...]).astype(BF16)
    o_ref[0] = _dot(xn, w_ref[0]).astype(BF16)


def _memkv(mem2d, g, wkv):
    rows, d = mem2d.shape
    depth, _, n = wkv.shape
    tm = min(ROW_TILE, rows)
    return pl.pallas_call(
        _memkv_kernel,
        out_shape=jax.ShapeDtypeStruct((depth, rows, n), BF16),
        grid=(depth, rows // tm),
        in_specs=[
            pl.BlockSpec((tm, d), lambda l, i: (i, 0)),
            _resident((1, d)),
            pl.BlockSpec((1, d, n), lambda l, i: (l, 0, 0)),
        ],
        out_specs=pl.BlockSpec((1, tm, n), lambda l, i: (l, i, 0)),
        compiler_params=_params(("parallel", "parallel"),
                                _vmem_limit(tm * d * 4 + d * n * 2 + tm * n * 2, d * 4, 2 * tm * n * 4)),
        name="mem_kv",
    )(mem2d, g, wkv)


def _even_in_kernel(x_ref, g_ref, w_ref, q_ref, k_ref, v_ref, u_ref):
    xn = _rms(x_ref[...], g_ref[...]).astype(BF16)
    w = A_HEADS * 2 * A_QK_DIM
    q = _dot(xn, w_ref[:, 0:w]) * (A_QK_DIM ** -0.5)
    first = (lax.broadcasted_iota(jnp.int32, q.shape, 1) % (2 * A_QK_DIM)) < A_QK_DIM
    q_ref[0] = jnp.where(first, q, 0.0).astype(BF16)
    q_ref[1] = jnp.where(first, 0.0, q).astype(BF16)
    k_ref[...] = _dot(xn, w_ref[:, w:2 * w]).astype(BF16)
    v_ref[...] = _dot(xn, w_ref[:, 2 * w:2 * w + A_WIDTH]).astype(BF16)
    u_ref[...] = _dot(xn, w_ref[:, 2 * w + A_WIDTH:]).astype(BF16)


def _even_in(h2d, g, w_in):
    rows, d = h2d.shape
    n = w_in.shape[1]
    c = n // 4
    tm = min(ROW_TILE, rows)
    row_spec = pl.BlockSpec((tm, c), lambda i: (i, 0))
    return pl.pallas_call(
        _even_in_kernel,
        out_shape=(jax.ShapeDtypeStruct((2, rows, c), BF16),) + (jax.ShapeDtypeStruct((rows, c), BF16),) * 3,
        grid=(rows // tm,),
        in_specs=[pl.BlockSpec((tm, d), lambda i: (i, 0)), _resident((1, d)), _resident((d, n))],
        out_specs=(pl.BlockSpec((2, tm, c), lambda i: (0, i, 0)), row_spec, row_spec, row_spec),
        compiler_params=_params(("parallel",),
                                _vmem_limit(tm * d * 4 + 5 * tm * c * 2, d * n * 2 + d * 4, 4 * tm * d * 4)),
        name="even_in",
    )(h2d, g, w_in)


def _attn_kernel(q_ref, k_ref, v_ref, bias_ref, lq_ref, g_ref, o_ref, m_sc, l_sc, acc_sc,
                 *, tile, lambda_init):
    i = pl.program_id(2)
    qq = q_ref[:, 0].reshape(2 * tile, 2 * A_QK_DIM)
    m_sc[...] = jnp.full(m_sc.shape, -jnp.inf, F32)
    l_sc[...] = jnp.zeros(l_sc.shape, F32)
    acc_sc[...] = jnp.zeros(acc_sc.shape, F32)

    def step(j, bias):
        start = pl.multiple_of(j * tile, tile)
        kj = k_ref[0, pl.ds(start, tile), :]
        vj = v_ref[0, pl.ds(start, tile), :]
        s = _dot_nt(qq, kj)
        if bias is not None:
            s = (s.reshape(2, tile, tile) + bias[None]).reshape(2 * tile, tile)
        m_prev = m_sc[...]
        m_new = jnp.maximum(m_prev, jnp.max(s, axis=-1, keepdims=True))
        alpha = jnp.exp(m_prev - m_new)
        p = jnp.exp(s - m_new)
        l_sc[...] = alpha * l_sc[...] + jnp.sum(p, axis=-1, keepdims=True)
        acc_sc[...] = alpha * acc_sc[...] + _dot(p.astype(BF16), vj)
        m_sc[...] = m_new

    def far(j, carry):
        step(j, None)
        return carry

    lax.fori_loop(0, jnp.maximum(i - 1, 0), far, 0)

    @pl.when(i >= 1)
    def _():
        step(i - 1, bias_ref[0, 1])

    step(i, bias_ref[0, 0])

    o = acc_sc[...] * (1.0 / l_sc[...])
    lq = lq_ref[...]
    lam = (jnp.exp(jnp.sum(lq[0:1] * lq[1:2], axis=-1, keepdims=True))
           - jnp.exp(jnp.sum(lq[2:3] * lq[3:4], axis=-1, keepdims=True)) + lambda_init)
    od = o[:tile] - lam * o[tile:]
    o_ref[0] = (_rms(od, g_ref[...]) * (1.0 - lambda_init)).astype(BF16)


def _diff_attention(q, k, v, bias_tiles, lq, g, lambda_init):
    _, b, s, _ = q.shape
    tile = bias_tiles.shape[-1]
    hd = 2 * A_QK_DIM
    kv_spec = pl.BlockSpec((1, s, hd), lambda bi, h, i: (bi, 0, h))
    pipelined = 2 * tile * hd * 2 + 2 * s * hd * 2 + 2 * tile * tile * 4 + tile * hd * 2
    scratch = 2 * (2 * tile) * 128 * 4 + 2 * tile * A_V_DIM * 4
    return pl.pallas_call(
        functools.partial(_attn_kernel, tile=tile, lambda_init=lambda_init),
        out_shape=jax.ShapeDtypeStruct((b, s, A_WIDTH), BF16),
        grid=(b, A_HEADS, s // tile),
        in_specs=[
            pl.BlockSpec((2, 1, tile, hd), lambda bi, h, i: (0, bi, i, h)),
            kv_spec,
            kv_spec,
            pl.BlockSpec((1, 2, tile, tile), lambda bi, h, i: (h, 0, 0, 0)),
            _resident(lq.shape),
            _resident(g.shape),
        ],
        out_specs=pl.BlockSpec((1, tile, A_V_DIM), lambda bi, h, i: (bi, i, h)),
        scratch_shapes=[
            pltpu.VMEM((2 * tile, 1), F32),
            pltpu.VMEM((2 * tile, 1), F32),
            pltpu.VMEM((2 * tile, A_V_DIM), F32),
        ],
        compiler_params=_params(("parallel", "parallel", "parallel"),
                                _vmem_limit(pipelined, 4096, scratch + 6 * (2 * tile) * tile * 4)),
        name="diff_attn",
    )(q, k, v, bias_tiles, lq, g)


def _even_out_kernel(h_ref, o_ref, u_ref, uh_ref, pw_ref, ps_ref, w_ref, out_ref, *, tm):
    i = pl.program_id(1)
    u = u_ref[0].astype(F32)
    halo = jnp.where(i > 0, uh_ref[0].astype(F32), 0.0)
    ext = jnp.concatenate([halo, u], axis=0)
    t = i * tm + lax.broadcasted_iota(jnp.int32, (tm, 1), 0)
    ys = []
    for gi, w in enumerate(POOL_WINDOWS):
        cols = slice(gi * POOL_GROUP_DIM, (gi + 1) * POOL_GROUP_DIM)
        a = ext[:, cols]
        sh = 1
        while sh < w:
            a = a + pltpu.roll(a, sh, 0)
            sh *= 2
        inv_count = 1.0 / jnp.minimum(t + 1, w).astype(F32)
        p = a[POOL_HALO:] * inv_count - u[:, cols]
        ys.append((_dot(p.astype(BF16), pw_ref[gi]) * ps_ref[:, cols]).astype(BF16))
    mix = jnp.concatenate([o_ref[0]] + ys, axis=-1)
    out_ref[0] = h_ref[0] + _dot(mix, w_ref[...])


def _even_out(h, o, u, pool_w, pool_scale, w_out):
    b, s, d = h.shape
    tm = min(ROW_TILE, s)
    halo_blocks = tm // POOL_HALO
    return pl.pallas_call(
        functools.partial(_even_out_kernel, tm=tm),
        out_shape=jax.ShapeDtypeStruct(h.shape, F32),
        grid=(b, s // tm),
        in_specs=[
            pl.BlockSpec((1, tm, d), lambda bi, i: (bi, i, 0)),
            pl.BlockSpec((1, tm, A_WIDTH), lambda bi, i: (bi, i, 0)),
            pl.BlockSpec((1, tm, POOL_WIDTH), lambda bi, i: (bi, i, 0)),
            pl.BlockSpec((1, POOL_HALO, POOL_WIDTH),
                         lambda bi, i: (bi, jnp.maximum(i * halo_blocks - 1, 0), 0)),
            _resident(pool_w.shape),
            _resident(pool_scale.shape),
            _resident(w_out.shape),
        ],
        out_specs=pl.BlockSpec((1, tm, d), lambda bi, i: (bi, i, 0)),
        compiler_params=_params(("parallel", "parallel"),
                                _vmem_limit(2 * tm * d * 4 + 2 * tm * 512 * 2 + POOL_HALO * 512 * 2,
                                            (pool_w.size + w_out.size) * 2 + pool_scale.size * 4,
                                            4 * tm * d * 4)),
        name="even_out",
    )(h, o, u, u, pool_w, pool_scale, w_out)


def _odd_kernel(h_ref, g_ref, win_ref, cw_ref, wout_ref, out_ref, zprev_sc, *, tm):
    i = pl.program_id(1)
    h = h_ref[0]
    d = h.shape[-1]
    hn = _rms(h, g_ref[...]).astype(BF16)
    b_gate = _dot(hn, win_ref[:, 0:d])
    z = _dot(hn, win_ref[:, d:2 * d]) * _dot(hn, win_ref[:, 2 * d:3 * d])

    @pl.when(i == 0)
    def _():
        zprev_sc[...] = jnp.zeros(zprev_sc.shape, F32)

    ext = jnp.concatenate([zprev_sc[...], z], axis=0)
    zprev_sc[...] = z[tm - CONV_HALO:]
    cw = cw_ref[...]
    y = z * cw[CONV_WIDTH - 1:CONV_WIDTH]
    for tap in range(CONV_WIDTH - 1):
        back = CONV_WIDTH - 1 - tap
        y = y + pltpu.roll(ext, back, 0)[CONV_HALO:] * cw[tap:tap + 1]
    out_ref[0] = h + _dot((b_gate * y).astype(BF16), wout_ref[...])


def _odd_mixer(h, g, w_in, conv_w, w_out):
    b, s, d = h.shape
    tm = min(ROW_TILE, s)
    return pl.pallas_call(
        functools.partial(_odd_kernel, tm=tm),
        out_shape=jax.ShapeDtypeStruct(h.shape, F32),
        grid=(b, s // tm),
        in_specs=[
            pl.BlockSpec((1, tm, d), lambda bi, i: (bi, i, 0)),
            _resident(g.shape),
            _resident(w_in.shape),
            _resident(conv_w.shape),
            _resident(w_out.shape),
        ],
        out_specs=pl.BlockSpec((1, tm, d), lambda bi, i: (bi, i, 0)),
        scratch_shapes=[pltpu.VMEM((CONV_HALO, d), F32)],
        compiler_params=_params(("arbitrary", "arbitrary"),
                                _vmem_limit(2 * tm * d * 4, (w_in.size + w_out.size) * 2,
                                            8 * tm * d * 4)),
        name="odd_mixer",
    )(h, g, w_in, conv_w, w_out)


def _xattn_kernel(h_ref, g_ref, wq_ref, kv_ref, wo_ref, out_ref):
    h = h_ref[0]
    d = h.shape[-1]
    hd = d // X_HEADS
    hn = _rms(h, g_ref[...]).astype(BF16)
    q = (_dot(hn, wq_ref[...]) * (hd ** -0.5)).astype(BF16)
    outs = []
    for head in range(X_HEADS):
        cols = slice(head * hd, (head + 1) * hd)
        s = _dot_nt(q[:, cols], kv_ref[0, 0, :, cols])
        p = jnp.exp(s - jnp.max(s, axis=-1, keepdims=True))
        inv_l = 1.0 / jnp.sum(p, axis=-1, keepdims=True)
        v = kv_ref[0, 0, :, d + head * hd:d + (head + 1) * hd]
        outs.append((_dot(p.astype(BF16), v) * inv_l).astype(BF16))
    out_ref[0] = h + _dot(jnp.concatenate(outs, axis=-1), wo_ref[...])


def _xattn(h, g, wq, kv, layer, wo):
    b, s, d = h.shape
    n_mem = kv.shape[2]
    tm = min(ROW_TILE, s)
    return pl.pallas_call(
        _xattn_kernel,
        out_shape=jax.ShapeDtypeStruct(h.shape, F32),
        grid=(b, s // tm),
        in_specs=[
            pl.BlockSpec((1, tm, d), lambda bi, i: (bi, i, 0)),
            _resident(g.shape),
            _resident(wq.shape),
            pl.BlockSpec((1, 1, n_mem, 2 * d), lambda bi, i: (layer, bi, 0, 0)),
            _resident(wo.shape),
        ],
        out_specs=pl.BlockSpec((1, tm, d), lambda bi, i: (bi, i, 0)),
        compiler_params=_params(("parallel", "parallel"),
                                _vmem_limit(2 * tm * d * 4 + n_mem * 2 * d * 2,
                                            (wq.size + wo.size) * 2, 6 * tm * d * 4)),
        name="xattn",
    )(h, g, wq, kv, wo)


def _mlp_kernel(h_ref, g_ref, w1_ref, w2_ref, gf_ref, out_ref, *, chunk, final_norm):
    h = h_ref[...]
    hn = _rms(h, g_ref[...]).astype(BF16)
    acc = h
    for c in range(w1_ref.shape[1] // chunk):
        a = jnp.maximum(_dot(hn, w1_ref[:, c * chunk:(c + 1) * chunk]), 0.0)
        acc = acc + _dot((a * a).astype(BF16), w2_ref[c * chunk:(c + 1) * chunk, :])
    out_ref[...] = _rms(acc, gf_ref[...]) if final_norm else acc


def _mlp(h2d, g, w1, w2, gf, final_norm):
    rows, d = h2d.shape
    tm = min(ROW_TILE, rows)
    chunk = min(1024, w1.shape[1])
    return pl.pallas_call(
        functools.partial(_mlp_kernel, chunk=chunk, final_norm=final_norm),
        out_shape=jax.ShapeDtypeStruct(h2d.shape, F32),
        grid=(rows // tm,),
        in_specs=[
            pl.BlockSpec((tm, d), lambda i: (i, 0)),
            _resident(g.shape),
            _resident(w1.shape),
            _resident(w2.shape),
            _resident(gf.shape),
        ],
        out_specs=pl.BlockSpec((tm, d), lambda i: (i, 0)),
        compiler_params=_params(("parallel",),
                                _vmem_limit(2 * tm * d * 4, (w1.size + w2.size) * 2,
                                            2 * tm * d * 4 + 3 * tm * chunk * 4)),
        name="mlp",
    )(h2d, g, w1, w2, gf)


def _t5_bucket(n):
    small = n < REL_MAX_EXACT
    nf = jnp.maximum(n, 1).astype(F32)
    large = REL_MAX_EXACT + (jnp.log(nf / REL_MAX_EXACT) / math.log(REL_MAX_DIST / REL_MAX_EXACT)
                             * (REL_BUCKETS - REL_MAX_EXACT)).astype(jnp.int32)
    return jnp.where(small, n, jnp.minimum(large, REL_BUCKETS - 1))


def _bias_tiles(rel_bias, tile):
    assert tile + 1 >= REL_MAX_DIST
    table = rel_bias.astype(F32)[_t5_bucket(jnp.arange(2 * tile, dtype=jnp.int32))].T
    table = table - rel_bias.astype(F32)[REL_BUCKETS - 1][:, None]
    r = jnp.arange(tile, dtype=jnp.int32)
    dist = r[:, None] - r[None, :]
    diag = jnp.where(dist >= 0, table[:, jnp.maximum(dist, 0)], NEG_BIG)
    prev = table[:, dist + tile]
    return jnp.stack([diag, prev], axis=1)


def kernel(x, mem, rel_bias, mem_norm_g, norm_mix_g, norm_xattn_g, norm_mlp_g, final_norm_g,
           ab_w_in, ab_w_out, lambda_q1, lambda_k1, lambda_q2, lambda_k2, subln_g, pool_w,
           pool_scale, conv_w_in, conv_w, conv_w_out, xattn_wq, xattn_wkv, xattn_wo, mlp_w1, mlp_w2):
    b, s, d = x.shape
    depth = norm_mix_g.shape[0]
    rows = b * s
    tile = min(ATTN_TILE, s)

    ab_w_in, ab_w_out, pool_w, conv_w_in, conv_w_out, xattn_wq, xattn_wkv, xattn_wo, mlp_w1, mlp_w2 = (
        w.astype(BF16) for w in (ab_w_in, ab_w_out, pool_w, conv_w_in, conv_w_out, xattn_wq,
                                 xattn_wkv, xattn_wo, mlp_w1, mlp_w2))

    bias_tiles = _bias_tiles(rel_bias, tile)
    kv = _memkv(mem.reshape(b * mem.shape[1], d), mem_norm_g.reshape(1, d), xattn_wkv)
    kv = kv.reshape(depth, b, mem.shape[1], 2 * d)

    h = x
    for l in range(depth):
        i = l // 2
        g_mix = norm_mix_g[l].reshape(1, d)
        if l % 2 == 0:
            lambda_init = 0.8 - 0.6 * math.exp(-0.3 * l)
            q, k, v, u = _even_in(h.reshape(rows, d), g_mix, ab_w_in[i])
            lq = jnp.stack([lambda_q1[i], lambda_k1[i], lambda_q2[i], lambda_k2[i]]).astype(F32)
            o = _diff_attention(q.reshape(2, b, s, -1), k.reshape(b, s, -1), v.reshape(b, s, -1),
                                bias_tiles, lq, subln_g[i].reshape(1, -1), lambda_init)
            h = _even_out(h, o, u.reshape(b, s, -1), pool_w[i], pool_scale[i].reshape(1, -1), ab_w_out[i])
        else:
            h = _odd_mixer(h, g_mix, conv_w_in[i], conv_w[i], conv_w_out[i])
        h = _xattn(h, norm_xattn_g[l].reshape(1, d), xattn_wq[l], kv, l, xattn_wo[l])
        h = _mlp(h.reshape(rows, d), norm_mlp_g[l].reshape(1, d), mlp_w1[l], mlp_w2[l],
                 final_norm_g.reshape(1, d), l == depth - 1).reshape(b, s, d)
    return h
```

```python
import functools
import math

import jax
import jax.numpy as jnp
from jax import lax
from jax.experimental import pallas as pl
from jax.experimental.pallas import tpu as pltpu

F32 = jnp.float32
BF16 = jnp.bfloat16

EPS = 1e-6
A_HEADS = 4
A_QK_DIM = 64
A_V_DIM = 2 * A_QK_DIM
A_WIDTH = A_HEADS * A_V_DIM
POOL_WINDOWS = (2, 4, 8, 16)
POOL_GROUP_DIM = 128
POOL_WIDTH = len(POOL_WINDOWS) * POOL_GROUP_DIM
POOL_HALO = 16
CONV_WIDTH = 3
CONV_HALO = 8
X_HEADS = 4
REL_BUCKETS = 32
REL_MAX_EXACT = REL_BUCKETS // 2
REL_MAX_DIST = 128

V7X_VMEM_LIMIT_CAP = 56 * 1024 * 1024

ROW_TILE = 512
ATTN_TILE = 512
NEG_BIG = -1e30


def _vmem_limit(pipelined_bytes, resident_bytes, live_bytes):
    return min(2 * pipelined_bytes + resident_bytes + live_bytes, V7X_VMEM_LIMIT_CAP)


def _params(semantics, vmem):
    return pltpu.CompilerParams(dimension_semantics=semantics, vmem_limit_bytes=vmem)


def _resident(shape):
    return pl.BlockSpec(shape, lambda *_: (0,) * len(shape), pipeline_mode=pl.Buffered(1))


def _rms(x, g):
    return x * lax.rsqrt(jnp.mean(x * x, axis=-1, keepdims=True) + EPS) * g


def _dot(a, b):
    return jnp.dot(a, b, preferred_element_type=F32)


def _dot_nt(a, b):
    return lax.dot_general(a, b, (((1,), (1,)), ((), ())), preferred_element_type=F32)


def _memkv_kernel(x_ref, g_ref, w_ref, o_ref):
    xn = _rms(x_ref[...], g_ref[...]).astype(BF16)
    o_ref[0] = _dot(xn, w_ref[0]).astype(BF16)


def _memkv(mem2d, g, wkv):
    rows, d = mem2d.shape
    depth, _, n = wkv.shape
    tm = min(ROW_TILE, rows)
    return pl.pallas_call(
        _memkv_kernel,
        out_shape=jax.ShapeDtypeStruct((depth, rows, n), BF16),
        grid=(depth, rows // tm),
        in_specs=[
            pl.BlockSpec((tm, d), lambda l, i: (i, 0)),
            _resident((1, d)),
            pl.BlockSpec((1, d, n), lambda l, i: (l, 0, 0)),
        ],
        out_specs=pl.BlockSpec((1, tm, n), lambda l, i: (l, i, 0)),
        compiler_params=_params(("parallel", "parallel"),
                                _vmem_limit(tm * d * 4 + d * n * 2 + tm * n * 2, d * 4, 2 * tm * n * 4)),
        name="mem_kv",
    )(mem2d, g, wkv)


def _even_in_kernel(x_ref, g_ref, w_ref, q_ref, k_ref, vt_ref, u_ref):
    xn = _rms(x_ref[...], g_ref[...]).astype(BF16)
    w = A_HEADS * 2 * A_QK_DIM
    q = _dot(xn, w_ref[:, 0:w]) * (A_QK_DIM ** -0.5)
    first = (lax.broadcasted_iota(jnp.int32, q.shape, 1) % (2 * A_QK_DIM)) < A_QK_DIM
    q_ref[0] = jnp.where(first, q, 0.0).astype(BF16)
    q_ref[1] = jnp.where(first, 0.0, q).astype(BF16)
    k_ref[...] = _dot(xn, w_ref[:, w:2 * w]).astype(BF16)
    vt = _dot(xn, w_ref[:, 2 * w:2 * w + A_WIDTH]).T.astype(BF16)
    vt_ref[0, :, 0] = vt.reshape(A_HEADS, A_V_DIM, vt.shape[-1])
    u_ref[...] = _dot(xn, w_ref[:, 2 * w + A_WIDTH:]).astype(BF16)


def _even_in(h2d, g, w_in, batch, tile):
    rows, d = h2d.shape
    n = w_in.shape[1]
    c = n // 4
    tiles = rows // batch // tile
    row_spec = pl.BlockSpec((tile, c), lambda i: (i, 0))
    return pl.pallas_call(
        _even_in_kernel,
        out_shape=(jax.ShapeDtypeStruct((2, rows, c), BF16),
                   jax.ShapeDtypeStruct((rows, c), BF16),
                   jax.ShapeDtypeStruct((batch, A_HEADS, tiles, A_V_DIM, tile), BF16),
                   jax.ShapeDtypeStruct((rows, c), BF16)),
        grid=(rows // tile,),
        in_specs=[pl.BlockSpec((tile, d), lambda i: (i, 0)), _resident((1, d)), _resident((d, n))],
        out_specs=(pl.BlockSpec((2, tile, c), lambda i: (0, i, 0)),
                   row_spec,
                   pl.BlockSpec((1, A_HEADS, 1, A_V_DIM, tile), lambda i: (i // tiles, 0, i % tiles, 0, 0)),
                   row_spec),
        compiler_params=_params(("parallel",),
                                _vmem_limit(tile * d * 4 + 5 * tile * c * 2, d * n * 2 + d * 4, 4 * tile * d * 4)),
        name="even_in",
    )(h2d, g, w_in)


def _attn_kernel(q_ref, k_ref, vt_ref, bias_ref, lq_ref, g_ref, o_ref, m_sc, l_sc, acc_sc,
                 *, tile, lambda_init):
    i = pl.program_id(2)
    qq = q_ref[:, 0].reshape(2 * tile, 2 * A_QK_DIM)
    m_sc[...] = jnp.full(m_sc.shape, -jnp.inf, F32)
    l_sc[...] = jnp.zeros(l_sc.shape, F32)
    acc_sc[...] = jnp.zeros(acc_sc.shape, F32)

    def step(j, bias):
        start = pl.multiple_of(j * tile, tile)
        kj = k_ref[0, pl.ds(start, tile), :]
        s = _dot_nt(kj, qq)
        if bias is not None:
            s = s + jnp.concatenate([bias, bias], axis=1)
        m_prev = m_sc[...]
        m_new = jnp.maximum(m_prev, jnp.max(s, axis=0, keepdims=True))
        alpha = jnp.exp(m_prev - m_new)
        p = jnp.exp(s - m_new)
        l_sc[...] = alpha * l_sc[...] + jnp.sum(p, axis=0, keepdims=True)
        acc_sc[...] = alpha * acc_sc[...] + _dot(vt_ref[0, 0, j], p.astype(BF16))
        m_sc[...] = m_new

    def far(j, carry):
        step(j, None)
        return carry

    lax.fori_loop(0, jnp.maximum(i - 1, 0), far, 0)

    @pl.when(i >= 1)
    def _():
        step(i - 1, bias_ref[0, 1])

    step(i, bias_ref[0, 0])

    o = acc_sc[...] * (1.0 / l_sc[...])
    lq = lq_ref[...]
    lam = (jnp.exp(jnp.sum(lq[0:1] * lq[1:2], axis=-1, keepdims=True))
           - jnp.exp(jnp.sum(lq[2:3] * lq[3:4], axis=-1, keepdims=True)) + lambda_init)
    od = o[:, :tile] - lam * o[:, tile:]
    y = od * lax.rsqrt(jnp.mean(od * od, axis=0, keepdims=True) + EPS) * g_ref[...]
    o_ref[0] = (y * (1.0 - lambda_init)).T.astype(BF16)


def _diff_attention(q, k, vt, bias_tiles, lq, g_col, lambda_init):
    _, b, s, _ = q.shape
    tile = bias_tiles.shape[-1]
    tiles = s // tile
    hd = 2 * A_QK_DIM
    pipelined = (2 * tile * hd * 2 + s * hd * 2 + s * A_V_DIM * 2 + 2 * tile * tile * 4
                 + tile * A_V_DIM * 2)
    scratch = 2 * 8 * (2 * tile) * 4 + A_V_DIM * 2 * tile * 4
    return pl.pallas_call(
        functools.partial(_attn_kernel, tile=tile, lambda_init=lambda_init),
        out_shape=jax.ShapeDtypeStruct((b, s, A_WIDTH), BF16),
        grid=(b, A_HEADS, tiles),
        in_specs=[
            pl.BlockSpec((2, 1, tile, hd), lambda bi, h, i: (0, bi, i, h)),
            pl.BlockSpec((1, s, hd), lambda bi, h, i: (bi, 0, h)),
            pl.BlockSpec((1, 1, tiles, A_V_DIM, tile), lambda bi, h, i: (bi, h, 0, 0, 0)),
            pl.BlockSpec((1, 2, tile, tile), lambda bi, h, i: (h, 0, 0, 0)),
            _resident(lq.shape),
            _resident(g_col.shape),
        ],
        out_specs=pl.BlockSpec((1, tile, A_V_DIM), lambda bi, h, i: (bi, i, h)),
        scratch_shapes=[
            pltpu.VMEM((1, 2 * tile), F32),
            pltpu.VMEM((1, 2 * tile), F32),
            pltpu.VMEM((A_V_DIM, 2 * tile), F32),
        ],
        compiler_params=_params(("parallel", "parallel", "parallel"),
                                _vmem_limit(pipelined, 128 * 128 * 4, scratch + 6 * tile * (2 * tile) * 4)),
        name="diff_attn",
    )(q, k, vt, bias_tiles, lq, g_col)


def _even_out_kernel(h_ref, o_ref, u_ref, uh_ref, pw_ref, ps_ref, w_ref, out_ref, *, tm):
    i = pl.program_id(1)
    u = u_ref[0].astype(F32)
    halo = jnp.where(i > 0, uh_ref[0].astype(F32), 0.0)
    ext = jnp.concatenate([halo, u], axis=0)
    t = i * tm + lax.broadcasted_iota(jnp.int32, (tm, 1), 0)
    ys = []
    for gi, w in enumerate(POOL_WINDOWS):
        cols = slice(gi * POOL_GROUP_DIM, (gi + 1) * POOL_GROUP_DIM)
        a = ext[:, cols]
        sh = 1
        while sh < w:
            a = a + pltpu.roll(a, sh, 0)
            sh *= 2
        inv_count = 1.0 / jnp.minimum(t + 1, w).astype(F32)
        p = a[POOL_HALO:] * inv_count - u[:, cols]
        ys.append((_dot(p.astype(BF16), pw_ref[gi]) * ps_ref[:, cols]).astype(BF16))
    mix = jnp.concatenate([o_ref[0]] + ys, axis=-1)
    out_ref[0] = h_ref[0] + _dot(mix, w_ref[...])


def _even_out(h, o, u, pool_w, pool_scale, w_out):
    b, s, d = h.shape
    tm = min(ROW_TILE, s)
    halo_blocks = tm // POOL_HALO
    return pl.pallas_call(
        functools.partial(_even_out_kernel, tm=tm),
        out_shape=jax.ShapeDtypeStruct(h.shape, F32),
        grid=(b, s // tm),
        in_specs=[
            pl.BlockSpec((1, tm, d), lambda bi, i: (bi, i, 0)),
            pl.BlockSpec((1, tm, A_WIDTH), lambda bi, i: (bi, i, 0)),
            pl.BlockSpec((1, tm, POOL_WIDTH), lambda bi, i: (bi, i, 0)),
            pl.BlockSpec((1, POOL_HALO, POOL_WIDTH),
                         lambda bi, i: (bi, jnp.maximum(i * halo_blocks - 1, 0), 0)),
            _resident(pool_w.shape),
            _resident(pool_scale.shape),
            _resident(w_out.shape),
        ],
        out_specs=pl.BlockSpec((1, tm, d), lambda bi, i: (bi, i, 0)),
        compiler_params=_params(("parallel", "parallel"),
                                _vmem_limit(2 * tm * d * 4 + 2 * tm * 512 * 2 + POOL_HALO * 512 * 2,
                                            (pool_w.size + w_out.size) * 2 + pool_scale.size * 4,
                                            4 * tm * d * 4)),
        name="even_out",
    )(h, o, u, u, pool_w, pool_scale, w_out)


def _odd_kernel(h_ref, g_ref, win_ref, cw_ref, wout_ref, out_ref, zprev_sc, *, tm):
    i = pl.program_id(1)
    h = h_ref[0]
    d = h.shape[-1]
    hn = _rms(h, g_ref[...]).astype(BF16)
    b_gate = _dot(hn, win_ref[:, 0:d])
    z = _dot(hn, win_ref[:, d:2 * d]) * _dot(hn, win_ref[:, 2 * d:3 * d])

    @pl.when(i == 0)
    def _():
        zprev_sc[...] = jnp.zeros(zprev_sc.shape, F32)

    ext = jnp.concatenate([zprev_sc[...], z], axis=0)
    zprev_sc[...] = z[tm - CONV_HALO:]
    cw = cw_ref[...]
    y = z * cw[CONV_WIDTH - 1:CONV_WIDTH]
    for tap in range(CONV_WIDTH - 1):
        back = CONV_WIDTH - 1 - tap
        y = y + pltpu.roll(ext, back, 0)[CONV_HALO:] * cw[tap:tap + 1]
    out_ref[0] = h + _dot((b_gate * y).astype(BF16), wout_ref[...])


def _odd_mixer(h, g, w_in, conv_w, w_out):
    b, s, d = h.shape
    tm = min(ROW_TILE, s)
    return pl.pallas_call(
        functools.partial(_odd_kernel, tm=tm),
        out_shape=jax.ShapeDtypeStruct(h.shape, F32),
        grid=(b, s // tm),
        in_specs=[
            pl.BlockSpec((1, tm, d), lambda bi, i: (bi, i, 0)),
            _resident(g.shape),
            _resident(w_in.shape),
            _resident(conv_w.shape),
            _resident(w_out.shape),
        ],
        out_specs=pl.BlockSpec((1, tm, d), lambda bi, i: (bi, i, 0)),
        scratch_shapes=[pltpu.VMEM((CONV_HALO, d), F32)],
        compiler_params=_params(("arbitrary", "arbitrary"),
                                _vmem_limit(2 * tm * d * 4, (w_in.size + w_out.size) * 2,
                                            8 * tm * d * 4)),
        name="odd_mixer",
    )(h, g, w_in, conv_w, w_out)


def _xattn_kernel(h_ref, g_ref, wq_ref, kv_ref, wo_ref, out_ref):
    h = h_ref[0]
    d = h.shape[-1]
    hd = d // X_HEADS
    hn = _rms(h, g_ref[...]).astype(BF16)
    q = (_dot(hn, wq_ref[...]) * (hd ** -0.5)).astype(BF16)
    outs = []
    for head in range(X_HEADS):
        cols = slice(head * hd, (head + 1) * hd)
        s = _dot_nt(q[:, cols], kv_ref[0, 0, :, cols])
        p = jnp.exp(s - jnp.max(s, axis=-1, keepdims=True))
        inv_l = 1.0 / jnp.sum(p, axis=-1, keepdims=True)
        v = kv_ref[0, 0, :, d + head * hd:d + (head + 1) * hd]
        outs.append((_dot(p.astype(BF16), v) * inv_l).astype(BF16))
    out_ref[0] = h + _dot(jnp.concatenate(outs, axis=-1), wo_ref[...])


def _xattn(h, g, wq, kv, layer, wo):
    b, s, d = h.shape
    n_mem = kv.shape[2]
    tm = min(ROW_TILE, s)
    return pl.pallas_call(
        _xattn_kernel,
        out_shape=jax.ShapeDtypeStruct(h.shape, F32),
        grid=(b, s // tm),
        in_specs=[
            pl.BlockSpec((1, tm, d), lambda bi, i: (bi, i, 0)),
            _resident(g.shape),
            _resident(wq.shape),
            pl.BlockSpec((1, 1, n_mem, 2 * d), lambda bi, i: (layer, bi, 0, 0)),
            _resident(wo.shape),
        ],
        out_specs=pl.BlockSpec((1, tm, d), lambda bi, i: (bi, i, 0)),
        compiler_params=_params(("parallel", "parallel"),
                                _vmem_limit(2 * tm * d * 4 + n_mem * 2 * d * 2,
                                            (wq.size + wo.size) * 2, 6 * tm * d * 4)),
        name="xattn",
    )(h, g, wq, kv, wo)


def _mlp_kernel(h_ref, g_ref, w1_ref, w2_ref, gf_ref, out_ref, *, chunk, final_norm):
    h = h_ref[...]
    hn = _rms(h, g_ref[...]).astype(BF16)
    acc = h
    for c in range(w1_ref.shape[1] // chunk):
        a = jnp.maximum(_dot(hn, w1_ref[:, c * chunk:(c + 1) * chunk]), 0.0)
        acc = acc + _dot((a * a).astype(BF16), w2_ref[c * chunk:(c + 1) * chunk, :])
    out_ref[...] = _rms(acc, gf_ref[...]) if final_norm else acc


def _mlp(h2d, g, w1, w2, gf, final_norm):
    rows, d = h2d.shape
    tm = min(ROW_TILE, rows)
    chunk = min(1024, w1.shape[1])
    return pl.pallas_call(
        functools.partial(_mlp_kernel, chunk=chunk, final_norm=final_norm),
        out_shape=jax.ShapeDtypeStruct(h2d.shape, F32),
        grid=(rows // tm,),
        in_specs=[
            pl.BlockSpec((tm, d), lambda i: (i, 0)),
            _resident(g.shape),
            _resident(w1.shape),
            _resident(w2.shape),
            _resident(gf.shape),
        ],
        out_specs=pl.BlockSpec((tm, d), lambda i: (i, 0)),
        compiler_params=_params(("parallel",),
                                _vmem_limit(2 * tm * d * 4, (w1.size + w2.size) * 2,
                                            2 * tm * d * 4 + 3 * tm * chunk * 4)),
        name="mlp",
    )(h2d, g, w1, w2, gf)


def _t5_bucket(n):
    small = n < REL_MAX_EXACT
    nf = jnp.maximum(n, 1).astype(F32)
    large = REL_MAX_EXACT + (jnp.log(nf / REL_MAX_EXACT) / math.log(REL_MAX_DIST / REL_MAX_EXACT)
                             * (REL_BUCKETS - REL_MAX_EXACT)).astype(jnp.int32)
    return jnp.where(small, n, jnp.minimum(large, REL_BUCKETS - 1))


def _bias_tiles(rel_bias, tile):
    assert tile + 1 >= REL_MAX_DIST
    rb = rel_bias.astype(F32)
    rb = (rb - rb[REL_BUCKETS - 1]).T[:, :, None, None]
    pos = jnp.arange(tile, dtype=jnp.int32)
    out = []
    for delta in (0, 1):
        dist = delta * tile + pos[None, :] - pos[:, None]
        bucket = _t5_bucket(jnp.maximum(dist, 0))
        vals = jnp.zeros((rb.shape[0], tile, tile), F32)
        for bkt in range(REL_BUCKETS - 1):
            vals = jnp.where(bucket == bkt, rb[:, bkt], vals)
        out.append(jnp.where(dist >= 0, vals, NEG_BIG))
    return jnp.stack(out, axis=1)


def kernel(x, mem, rel_bias, mem_norm_g, norm_mix_g, norm_xattn_g, norm_mlp_g, final_norm_g,
           ab_w_in, ab_w_out, lambda_q1, lambda_k1, lambda_q2, lambda_k2, subln_g, pool_w,
           pool_scale, conv_w_in, conv_w, conv_w_out, xattn_wq, xattn_wkv, xattn_wo, mlp_w1, mlp_w2):
    b, s, d = x.shape
    depth = norm_mix_g.shape[0]
    rows = b * s
    tile = min(ATTN_TILE, s)

    ab_w_in, ab_w_out, pool_w, conv_w_in, conv_w_out, xattn_wq, xattn_wkv, xattn_wo, mlp_w1, mlp_w2 = (
        w.astype(BF16) for w in (ab_w_in, ab_w_out, pool_w, conv_w_in, conv_w_out, xattn_wq,
                                 xattn_wkv, xattn_wo, mlp_w1, mlp_w2))

    bias_tiles = _bias_tiles(rel_bias, tile)
    kv = _memkv(mem.reshape(b * mem.shape[1], d), mem_norm_g.reshape(1, d), xattn_wkv)
    kv = kv.reshape(depth, b, mem.shape[1], 2 * d)

    h = x
    for l in range(depth):
        i = l // 2
        g_mix = norm_mix_g[l].reshape(1, d)
        if l % 2 == 0:
            lambda_init = 0.8 - 0.6 * math.exp(-0.3 * l)
            q, k, vt, u = _even_in(h.reshape(rows, d), g_mix, ab_w_in[i], b, tile)
            lq = jnp.stack([lambda_q1[i], lambda_k1[i], lambda_q2[i], lambda_k2[i]]).astype(F32)
            o = _diff_attention(q.reshape(2, b, s, -1), k.reshape(b, s, -1), vt,
                                bias_tiles, lq, subln_g[i].reshape(-1, 1), lambda_init)
            h = _even_out(h, o, u.reshape(b, s, -1), pool_w[i], pool_scale[i].reshape(1, -1), ab_w_out[i])
        else:
            h = _odd_mixer(h, g_mix, conv_w_in[i], conv_w[i], conv_w_out[i])
        h = _xattn(h, norm_xattn_g[l].reshape(1, d), xattn_wq[l], kv, l, xattn_wo[l])
        h = _mlp(h.reshape(rows, d), norm_mlp_g[l].reshape(1, d), mlp_w1[l], mlp_w2[l],
                 final_norm_g.reshape(1, d), l == depth - 1).reshape(b, s, d)
    return h
```

```python
import functools
import math

import jax
import jax.numpy as jnp
from jax import lax
from jax.experimental import pallas as pl
from jax.experimental.pallas import tpu as pltpu

F32 = jnp.float32
BF16 = jnp.bfloat16

EPS = 1e-6
A_HEADS = 4
A_QK_DIM = 64
A_V_DIM = 2 * A_QK_DIM
A_WIDTH = A_HEADS * A_V_DIM
POOL_WINDOWS = (2, 4, 8, 16)
POOL_GROUP_DIM = 128
POOL_WIDTH = len(POOL_WINDOWS) * POOL_GROUP_DIM
POOL_HALO = 16
CONV_WIDTH = 3
CONV_HALO = 8
X_HEADS = 4
REL_BUCKETS = 32
REL_MAX_EXACT = REL_BUCKETS // 2
REL_MAX_DIST = 128

V7X_VMEM_LIMIT_CAP = 56 * 1024 * 1024

ROW_TILE = 512
ATTN_TILE = 512
ATTN_QUERY_CHUNK = 256
LOG2E = math.log2(math.e)
NEG_BIG = -1e30


def _vmem_limit(pipelined_bytes, resident_bytes, live_bytes):
    return min(2 * pipelined_bytes + resident_bytes + live_bytes, V7X_VMEM_LIMIT_CAP)


def _params(semantics, vmem):
    return pltpu.CompilerParams(dimension_semantics=semantics, vmem_limit_bytes=vmem)


def _resident(shape):
    return pl.BlockSpec(shape, lambda *_: (0,) * len(shape), pipeline_mode=pl.Buffered(1))


def _rms(x, g):
    return x * lax.rsqrt(jnp.mean(x * x, axis=-1, keepdims=True) + EPS) * g


def _dot(a, b):
    return jnp.dot(a, b, preferred_element_type=F32)


def _dot_nt(a, b):
    return lax.dot_general(a, b, (((1,), (1,)), ((), ())), preferred_element_type=F32)


def _memkv_kernel(x_ref, g_ref, w_ref, o_ref):
    xn = _rms(x_ref[...], g_ref[...]).astype(BF16)
    o_ref[0] = _dot(xn, w_ref[0]).astype(BF16)


def _memkv(mem2d, g, wkv):
    rows, d = mem2d.shape
    depth, _, n = wkv.shape
    tm = min(ROW_TILE, rows)
    return pl.pallas_call(
        _memkv_kernel,
        out_shape=jax.ShapeDtypeStruct((depth, rows, n), BF16),
        grid=(depth, rows // tm),
        in_specs=[
            pl.BlockSpec((tm, d), lambda l, i: (i, 0)),
            _resident((1, d)),
            pl.BlockSpec((1, d, n), lambda l, i: (l, 0, 0)),
        ],
        out_specs=pl.BlockSpec((1, tm, n), lambda l, i: (l, i, 0)),
        compiler_params=_params(("parallel", "parallel"),
                                _vmem_limit(tm * d * 4 + d * n * 2 + tm * n * 2, d * 4, 2 * tm * n * 4)),
        name="mem_kv",
    )(mem2d, g, wkv)


def _even_in_kernel(x_ref, g_ref, w_ref, q_ref, k_ref, vt_ref, u_ref):
    xn = _rms(x_ref[...], g_ref[...]).astype(BF16)
    w = A_HEADS * 2 * A_QK_DIM
    q = _dot(xn, w_ref[:, 0:w]) * (A_QK_DIM ** -0.5 * LOG2E)
    first = (lax.broadcasted_iota(jnp.int32, q.shape, 1) % (2 * A_QK_DIM)) < A_QK_DIM
    q_ref[0] = jnp.where(first, q, 0.0).astype(BF16)
    q_ref[1] = jnp.where(first, 0.0, q).astype(BF16)
    k_ref[...] = _dot(xn, w_ref[:, w:2 * w]).astype(BF16)
    vt = _dot(xn, w_ref[:, 2 * w:2 * w + A_WIDTH]).T.astype(BF16)
    vt_ref[0, :, 0] = vt.reshape(A_HEADS, A_V_DIM, vt.shape[-1])
    u_ref[...] = _dot(xn, w_ref[:, 2 * w + A_WIDTH:]).astype(BF16)


def _even_in(h2d, g, w_in, batch, tile):
    rows, d = h2d.shape
    n = w_in.shape[1]
    c = n // 4
    tiles = rows // batch // tile
    row_spec = pl.BlockSpec((tile, c), lambda i: (i, 0))
    return pl.pallas_call(
        _even_in_kernel,
        out_shape=(jax.ShapeDtypeStruct((2, rows, c), BF16),
                   jax.ShapeDtypeStruct((rows, c), BF16),
                   jax.ShapeDtypeStruct((batch, A_HEADS, tiles, A_V_DIM, tile), BF16),
                   jax.ShapeDtypeStruct((rows, c), BF16)),
        grid=(rows // tile,),
        in_specs=[pl.BlockSpec((tile, d), lambda i: (i, 0)), _resident((1, d)), _resident((d, n))],
        out_specs=(pl.BlockSpec((2, tile, c), lambda i: (0, i, 0)),
                   row_spec,
                   pl.BlockSpec((1, A_HEADS, 1, A_V_DIM, tile), lambda i: (i // tiles, 0, i % tiles, 0, 0)),
                   row_spec),
        compiler_params=_params(("parallel",),
                                _vmem_limit(tile * d * 4 + 5 * tile * c * 2, d * n * 2 + d * 4, 4 * tile * d * 4)),
        name="even_in",
    )(h2d, g, w_in)


def _attn_kernel(q_ref, k_ref, vt_ref, bias_ref, lq_ref, g_ref, o_ref, s_sc, m_sc, l_sc, acc_sc,
                 *, tile, lambda_init):
    i = pl.program_id(2)
    qq = q_ref[:, 0].reshape(2 * tile, 2 * A_QK_DIM)
    m_sc[...] = jnp.full(m_sc.shape, -jnp.inf, F32)
    l_sc[...] = jnp.zeros(l_sc.shape, F32)
    acc_sc[...] = jnp.zeros(acc_sc.shape, F32)

    n_chunks = 2 * tile // ATTN_QUERY_CHUNK
    chunk_cols = [slice(c * ATTN_QUERY_CHUNK, (c + 1) * ATTN_QUERY_CHUNK) for c in range(n_chunks)]

    def produce(c, j, bias, dst):
        start = pl.multiple_of(j * tile, tile)
        s = _dot_nt(k_ref[0, pl.ds(start, tile), :], qq[chunk_cols[c]])
        if bias is not None:
            b0 = (c * ATTN_QUERY_CHUNK) % tile
            s = s + bias[:, b0:b0 + ATTN_QUERY_CHUNK]
        dst[:, chunk_cols[c]] = s

    def consume(c, j, src):
        cols = chunk_cols[c]
        s = src[:, cols]
        m_prev = m_sc[:, cols]
        m_new = jnp.maximum(m_prev, jnp.max(s, axis=0, keepdims=True))
        alpha = jnp.exp2(m_prev - m_new)
        p = jnp.exp2(s - m_new)
        l_sc[:, cols] = alpha * l_sc[:, cols] + jnp.sum(p, axis=0, keepdims=True)
        acc_sc[:, cols] = alpha * acc_sc[:, cols] + _dot(vt_ref[0, 0, j], p.astype(BF16))
        m_sc[:, cols] = m_new

    def block(cons, prod):
        for c in range(n_chunks):
            if prod is not None:
                produce(c, *prod)
            if cons is not None:
                consume(c, *cons)

    buf_a, buf_b = s_sc.at[0], s_sc.at[1]
    block(None, (i, bias_ref.at[0, 0], buf_a))

    @pl.when(i == 0)
    def _():
        block((i, buf_a), None)

    @pl.when(i >= 1)
    def _():
        block((i, buf_a), (i - 1, bias_ref.at[0, 1], buf_b))

    @pl.when(i == 1)
    def _():
        block((i - 1, buf_b), None)

    @pl.when(i >= 2)
    def _():
        block((i - 1, buf_b), (0, None, buf_a))

    def far_pair(t, carry):
        j = 2 * t
        block((j, buf_a), (j + 1, None, buf_b))
        block((j + 1, buf_b), (j + 2, None, buf_a))
        return carry

    n_far = i - 1
    n_pairs = jnp.maximum(n_far - 1, 0) // 2
    lax.fori_loop(0, n_pairs, far_pair, 0)
    j_tail = 2 * n_pairs

    @pl.when(jnp.logical_and(n_far >= 1, n_far - j_tail == 1))
    def _():
        block((j_tail, buf_a), None)

    @pl.when(n_far - j_tail == 2)
    def _():
        block((j_tail, buf_a), (j_tail + 1, None, buf_b))
        block((j_tail + 1, buf_b), None)

    o = acc_sc[...] * (1.0 / l_sc[...])
    lq = lq_ref[...]
    lam = (jnp.exp(jnp.sum(lq[0:1] * lq[1:2], axis=-1, keepdims=True))
           - jnp.exp(jnp.sum(lq[2:3] * lq[3:4], axis=-1, keepdims=True)) + lambda_init)
    od = o[:, :tile] - lam * o[:, tile:]
    y = od * lax.rsqrt(jnp.mean(od * od, axis=0, keepdims=True) + EPS) * g_ref[...]
    o_ref[0] = (y * (1.0 - lambda_init)).T.astype(BF16)


def _diff_attention(q, k, vt, bias_tiles, lq, g_col, lambda_init):
    _, b, s, _ = q.shape
    tile = bias_tiles.shape[-1]
    tiles = s // tile
    hd = 2 * A_QK_DIM
    pipelined = (2 * tile * hd * 2 + s * hd * 2 + s * A_V_DIM * 2 + 2 * tile * tile * 4
                 + tile * A_V_DIM * 2)
    scratch = 2 * tile * 2 * tile * 4 + 2 * 8 * (2 * tile) * 4 + A_V_DIM * 2 * tile * 4
    return pl.pallas_call(
        functools.partial(_attn_kernel, tile=tile, lambda_init=lambda_init),
        out_shape=jax.ShapeDtypeStruct((b, s, A_WIDTH), BF16),
        grid=(b, A_HEADS, tiles),
        in_specs=[
            pl.BlockSpec((2, 1, tile, hd), lambda bi, h, i: (0, bi, i, h)),
            pl.BlockSpec((1, s, hd), lambda bi, h, i: (bi, 0, h)),
            pl.BlockSpec((1, 1, tiles, A_V_DIM, tile), lambda bi, h, i: (bi, h, 0, 0, 0)),
            pl.BlockSpec((1, 2, tile, tile), lambda bi, h, i: (h, 0, 0, 0)),
            _resident(lq.shape),
            _resident(g_col.shape),
        ],
        out_specs=pl.BlockSpec((1, tile, A_V_DIM), lambda bi, h, i: (bi, i, h)),
        scratch_shapes=[
            pltpu.VMEM((2, tile, 2 * tile), F32),
            pltpu.VMEM((1, 2 * tile), F32),
            pltpu.VMEM((1, 2 * tile), F32),
            pltpu.VMEM((A_V_DIM, 2 * tile), F32),
        ],
        compiler_params=_params(("parallel", "parallel", "parallel"),
                                _vmem_limit(pipelined, 128 * 128 * 4, scratch + 6 * tile * (2 * tile) * 4)),
        name="diff_attn",
    )(q, k, vt, bias_tiles, lq, g_col)


def _even_out_kernel(h_ref, o_ref, u_ref, uh_ref, pw_ref, ps_ref, w_ref, out_ref, *, tm):
    i = pl.program_id(1)
    u = u_ref[0].astype(F32)
    halo = jnp.where(i > 0, uh_ref[0].astype(F32), 0.0)
    ext = jnp.concatenate([halo, u], axis=0)
    t = i * tm + lax.broadcasted_iota(jnp.int32, (tm, 1), 0)
    ys = []
    for gi, w in enumerate(POOL_WINDOWS):
        cols = slice(gi * POOL_GROUP_DIM, (gi + 1) * POOL_GROUP_DIM)
        a = ext[:, cols]
        sh = 1
        while sh < w:
            a = a + pltpu.roll(a, sh, 0)
            sh *= 2
        inv_count = 1.0 / jnp.minimum(t + 1, w).astype(F32)
        p = a[POOL_HALO:] * inv_count - u[:, cols]
        ys.append((_dot(p.astype(BF16), pw_ref[gi]) * ps_ref[:, cols]).astype(BF16))
    mix = jnp.concatenate([o_ref[0]] + ys, axis=-1)
    out_ref[0] = h_ref[0] + _dot(mix, w_ref[...])


def _even_out(h, o, u, pool_w, pool_scale, w_out):
    b, s, d = h.shape
    tm = min(ROW_TILE, s)
    halo_blocks = tm // POOL_HALO
    return pl.pallas_call(
        functools.partial(_even_out_kernel, tm=tm),
        out_shape=jax.ShapeDtypeStruct(h.shape, F32),
        grid=(b, s // tm),
        in_specs=[
            pl.BlockSpec((1, tm, d), lambda bi, i: (bi, i, 0)),
            pl.BlockSpec((1, tm, A_WIDTH), lambda bi, i: (bi, i, 0)),
            pl.BlockSpec((1, tm, POOL_WIDTH), lambda bi, i: (bi, i, 0)),
            pl.BlockSpec((1, POOL_HALO, POOL_WIDTH),
                         lambda bi, i: (bi, jnp.maximum(i * halo_blocks - 1, 0), 0)),
            _resident(pool_w.shape),
            _resident(pool_scale.shape),
            _resident(w_out.shape),
        ],
        out_specs=pl.BlockSpec((1, tm, d), lambda bi, i: (bi, i, 0)),
        compiler_params=_params(("parallel", "parallel"),
                                _vmem_limit(2 * tm * d * 4 + 2 * tm * 512 * 2 + POOL_HALO * 512 * 2,
                                            (pool_w.size + w_out.size) * 2 + pool_scale.size * 4,
                                            4 * tm * d * 4)),
        name="even_out",
    )(h, o, u, u, pool_w, pool_scale, w_out)


def _odd_kernel(h_ref, g_ref, win_ref, cw_ref, wout_ref, out_ref, zprev_sc, *, tm):
    i = pl.program_id(1)
    h = h_ref[0]
    d = h.shape[-1]
    hn = _rms(h, g_ref[...]).astype(BF16)
    b_gate = _dot(hn, win_ref[:, 0:d])
    z = _dot(hn, win_ref[:, d:2 * d]) * _dot(hn, win_ref[:, 2 * d:3 * d])

    @pl.when(i == 0)
    def _():
        zprev_sc[...] = jnp.zeros(zprev_sc.shape, F32)

    ext = jnp.concatenate([zprev_sc[...], z], axis=0)
    zprev_sc[...] = z[tm - CONV_HALO:]
    cw = cw_ref[...]
    y = z * cw[CONV_WIDTH - 1:CONV_WIDTH]
    for tap in range(CONV_WIDTH - 1):
        back = CONV_WIDTH - 1 - tap
        y = y + pltpu.roll(ext, back, 0)[CONV_HALO:] * cw[tap:tap + 1]
    out_ref[0] = h + _dot((b_gate * y).astype(BF16), wout_ref[...])


def _odd_mixer(h, g, w_in, conv_w, w_out):
    b, s, d = h.shape
    tm = min(ROW_TILE, s)
    return pl.pallas_call(
        functools.partial(_odd_kernel, tm=tm),
        out_shape=jax.ShapeDtypeStruct(h.shape, F32),
        grid=(b, s // tm),
        in_specs=[
            pl.BlockSpec((1, tm, d), lambda bi, i: (bi, i, 0)),
            _resident(g.shape),
            _resident(w_in.shape),
            _resident(conv_w.shape),
            _resident(w_out.shape),
        ],
        out_specs=pl.BlockSpec((1, tm, d), lambda bi, i: (bi, i, 0)),
        scratch_shapes=[pltpu.VMEM((CONV_HALO, d), F32)],
        compiler_params=_params(("arbitrary", "arbitrary"),
                                _vmem_limit(2 * tm * d * 4, (w_in.size + w_out.size) * 2,
                                            8 * tm * d * 4)),
        name="odd_mixer",
    )(h, g, w_in, conv_w, w_out)


def _xattn_kernel(h_ref, g_ref, wq_ref, kv_ref, wo_ref, out_ref):
    h = h_ref[0]
    d = h.shape[-1]
    hd = d // X_HEADS
    hn = _rms(h, g_ref[...]).astype(BF16)
    q = (_dot(hn, wq_ref[...]) * (hd ** -0.5)).astype(BF16)
    outs = []
    for head in range(X_HEADS):
        cols = slice(head * hd, (head + 1) * hd)
        s = _dot_nt(q[:, cols], kv_ref[0, 0, :, cols])
        p = jnp.exp(s - jnp.max(s, axis=-1, keepdims=True))
        inv_l = 1.0 / jnp.sum(p, axis=-1, keepdims=True)
        v = kv_ref[0, 0, :, d + head * hd:d + (head + 1) * hd]
        outs.append((_dot(p.astype(BF16), v) * inv_l).astype(BF16))
    out_ref[0] = h + _dot(jnp.concatenate(outs, axis=-1), wo_ref[...])


def _xattn(h, g, wq, kv, layer, wo):
    b, s, d = h.shape
    n_mem = kv.shape[2]
    tm = min(ROW_TILE, s)
    return pl.pallas_call(
        _xattn_kernel,
        out_shape=jax.ShapeDtypeStruct(h.shape, F32),
        grid=(b, s // tm),
        in_specs=[
            pl.BlockSpec((1, tm, d), lambda bi, i: (bi, i, 0)),
            _resident(g.shape),
            _resident(wq.shape),
            pl.BlockSpec((1, 1, n_mem, 2 * d), lambda bi, i: (layer, bi, 0, 0)),
            _resident(wo.shape),
        ],
        out_specs=pl.BlockSpec((1, tm, d), lambda bi, i: (bi, i, 0)),
        compiler_params=_params(("parallel", "parallel"),
                                _vmem_limit(2 * tm * d * 4 + n_mem * 2 * d * 2,
                                            (wq.size + wo.size) * 2, 6 * tm * d * 4)),
        name="xattn",
    )(h, g, wq, kv, wo)


def _mlp_kernel(h_ref, g_ref, w1_ref, w2_ref, gf_ref, out_ref, *, chunk, final_norm):
    h = h_ref[...]
    hn = _rms(h, g_ref[...]).astype(BF16)
    acc = h
    for c in range(w1_ref.shape[1] // chunk):
        a = jnp.maximum(_dot(hn, w1_ref[:, c * chunk:(c + 1) * chunk]), 0.0)
        acc = acc + _dot((a * a).astype(BF16), w2_ref[c * chunk:(c + 1) * chunk, :])
    out_ref[...] = _rms(acc, gf_ref[...]) if final_norm else acc


def _mlp(h2d, g, w1, w2, gf, final_norm):
    rows, d = h2d.shape
    tm = min(ROW_TILE, rows)
    chunk = min(1024, w1.shape[1])
    return pl.pallas_call(
        functools.partial(_mlp_kernel, chunk=chunk, final_norm=final_norm),
        out_shape=jax.ShapeDtypeStruct(h2d.shape, F32),
        grid=(rows // tm,),
        in_specs=[
            pl.BlockSpec((tm, d), lambda i: (i, 0)),
            _resident(g.shape),
            _resident(w1.shape),
            _resident(w2.shape),
            _resident(gf.shape),
        ],
        out_specs=pl.BlockSpec((tm, d), lambda i: (i, 0)),
        compiler_params=_params(("parallel",),
                                _vmem_limit(2 * tm * d * 4, (w1.size + w2.size) * 2,
                                            2 * tm * d * 4 + 3 * tm * chunk * 4)),
        name="mlp",
    )(h2d, g, w1, w2, gf)


def _t5_bucket(n):
    small = n < REL_MAX_EXACT
    nf = jnp.maximum(n, 1).astype(F32)
    large = REL_MAX_EXACT + (jnp.log(nf / REL_MAX_EXACT) / math.log(REL_MAX_DIST / REL_MAX_EXACT)
                             * (REL_BUCKETS - REL_MAX_EXACT)).astype(jnp.int32)
    return jnp.where(small, n, jnp.minimum(large, REL_BUCKETS - 1))


def _bias_tiles(rel_bias, tile):
    assert tile + 1 >= REL_MAX_DIST
    rb = rel_bias.astype(F32)
    rb = (rb - rb[REL_BUCKETS - 1]).T[:, :, None, None]
    pos = jnp.arange(tile, dtype=jnp.int32)
    out = []
    for delta in (0, 1):
        dist = delta * tile + pos[None, :] - pos[:, None]
        bucket = _t5_bucket(jnp.maximum(dist, 0))
        vals = jnp.zeros((rb.shape[0], tile, tile), F32)
        for bkt in range(REL_BUCKETS - 1):
            vals = jnp.where(bucket == bkt, rb[:, bkt], vals)
        out.append(jnp.where(dist >= 0, vals * LOG2E, NEG_BIG))
    return jnp.stack(out, axis=1)


def kernel(x, mem, rel_bias, mem_norm_g, norm_mix_g, norm_xattn_g, norm_mlp_g, final_norm_g,
           ab_w_in, ab_w_out, lambda_q1, lambda_k1, lambda_q2, lambda_k2, subln_g, pool_w,
           pool_scale, conv_w_in, conv_w, conv_w_out, xattn_wq, xattn_wkv, xattn_wo, mlp_w1, mlp_w2):
    b, s, d = x.shape
    depth = norm_mix_g.shape[0]
    rows = b * s
    tile = min(ATTN_TILE, s)

    ab_w_in, ab_w_out, pool_w, conv_w_in, conv_w_out, xattn_wq, xattn_wkv, xattn_wo, mlp_w1, mlp_w2 = (
        w.astype(BF16) for w in (ab_w_in, ab_w_out, pool_w, conv_w_in, conv_w_out, xattn_wq,
                                 xattn_wkv, xattn_wo, mlp_w1, mlp_w2))

    bias_tiles = _bias_tiles(rel_bias, tile)
    kv = _memkv(mem.reshape(b * mem.shape[1], d), mem_norm_g.reshape(1, d), xattn_wkv)
    kv = kv.reshape(depth, b, mem.shape[1], 2 * d)

    h = x
    for l in range(depth):
        i = l // 2
        g_mix = norm_mix_g[l].reshape(1, d)
        if l % 2 == 0:
            lambda_init = 0.8 - 0.6 * math.exp(-0.3 * l)
            q, k, vt, u = _even_in(h.reshape(rows, d), g_mix, ab_w_in[i], b, tile)
            lq = jnp.stack([lambda_q1[i], lambda_k1[i], lambda_q2[i], lambda_k2[i]]).astype(F32)
            o = _diff_attention(q.reshape(2, b, s, -1), k.reshape(b, s, -1), vt,
                                bias_tiles, lq, subln_g[i].reshape(-1, 1), lambda_init)
            h = _even_out(h, o, u.reshape(b, s, -1), pool_w[i], pool_scale[i].reshape(1, -1), ab_w_out[i])
        else:
            h = _odd_mixer(h, g_mix, conv_w_in[i], conv_w[i], conv_w_out[i])
        h = _xattn(h, norm_xattn_g[l].reshape(1, d), xattn_wq[l], kv, l, xattn_wo[l])
        h = _mlp(h.reshape(rows, d), norm_mlp_g[l].reshape(1, d), mlp_w1[l], mlp_w2[l],
                 final_norm_g.reshape(1, d), l == depth - 1).reshape(b, s, d)
    return h
```

```python
import functools
import math

import jax
import jax.numpy as jnp
from jax import lax
from jax.experimental import pallas as pl
from jax.experimental.pallas import tpu as pltpu

F32 = jnp.float32
BF16 = jnp.bfloat16

EPS = 1e-6
A_HEADS = 4
A_QK_DIM = 64
A_V_DIM = 2 * A_QK_DIM
A_WIDTH = A_HEADS * A_V_DIM
A_VT_ROWS = A_V_DIM + 16
POOL_WINDOWS = (2, 4, 8, 16)
POOL_GROUP_DIM = 128
POOL_WIDTH = len(POOL_WINDOWS) * POOL_GROUP_DIM
POOL_HALO = 16
CONV_WIDTH = 3
CONV_HALO = 8
X_HEADS = 4
REL_BUCKETS = 32
REL_MAX_EXACT = REL_BUCKETS // 2
REL_MAX_DIST = 128

V7X_VMEM_LIMIT_CAP = 56 * 1024 * 1024

ROW_TILE = 512
ATTN_TILE = 512
ATTN_QUERY_CHUNK = 256
LOG2E = math.log2(math.e)
NEG_BIG = -1e30


def _vmem_limit(pipelined_bytes, resident_bytes, live_bytes):
    return min(2 * pipelined_bytes + resident_bytes + live_bytes, V7X_VMEM_LIMIT_CAP)


def _params(semantics, vmem):
    return pltpu.CompilerParams(dimension_semantics=semantics, vmem_limit_bytes=vmem)


def _resident(shape):
    return pl.BlockSpec(shape, lambda *_: (0,) * len(shape), pipeline_mode=pl.Buffered(1))


def _rms(x, g):
    return x * lax.rsqrt(jnp.mean(x * x, axis=-1, keepdims=True) + EPS) * g


def _dot(a, b):
    return jnp.dot(a, b, preferred_element_type=F32)


def _dot_nt(a, b):
    return lax.dot_general(a, b, (((1,), (1,)), ((), ())), preferred_element_type=F32)


def _memkv_kernel(x_ref, g_ref, w_ref, o_ref):
    xn = _rms(x_ref[...], g_ref[...]).astype(BF16)
    o_ref[0] = _dot(xn, w_ref[0]).astype(BF16)


def _memkv(mem2d, g, wkv):
    rows, d = mem2d.shape
    depth, _, n = wkv.shape
    tm = min(ROW_TILE, rows)
    return pl.pallas_call(
        _memkv_kernel,
        out_shape=jax.ShapeDtypeStruct((depth, rows, n), BF16),
        grid=(depth, rows // tm),
        in_specs=[
            pl.BlockSpec((tm, d), lambda l, i: (i, 0)),
            _resident((1, d)),
            pl.BlockSpec((1, d, n), lambda l, i: (l, 0, 0)),
        ],
        out_specs=pl.BlockSpec((1, tm, n), lambda l, i: (l, i, 0)),
        compiler_params=_params(("parallel", "parallel"),
                                _vmem_limit(tm * d * 4 + d * n * 2 + tm * n * 2, d * 4, 2 * tm * n * 4)),
        name="mem_kv",
    )(mem2d, g, wkv)


def _even_in_kernel(x_ref, g_ref, w_ref, q_ref, k_ref, vt_ref, u_ref):
    xn = _rms(x_ref[...], g_ref[...]).astype(BF16)
    w = A_HEADS * 2 * A_QK_DIM
    q = _dot(xn, w_ref[:, 0:w]) * (A_QK_DIM ** -0.5 * LOG2E)
    first = (lax.broadcasted_iota(jnp.int32, q.shape, 1) % (2 * A_QK_DIM)) < A_QK_DIM
    q_ref[0] = jnp.where(first, q, 0.0).astype(BF16)
    q_ref[1] = jnp.where(first, 0.0, q).astype(BF16)
    k_ref[...] = _dot(xn, w_ref[:, w:2 * w]).astype(BF16)
    vt = _dot(xn, w_ref[:, 2 * w:2 * w + A_WIDTH]).T.astype(BF16)
    vt_ref[0, :, 0, 0:A_V_DIM] = vt.reshape(A_HEADS, A_V_DIM, vt.shape[-1])
    pad_shape = (A_HEADS, A_VT_ROWS - A_V_DIM, vt.shape[-1])
    ones_row = lax.broadcasted_iota(jnp.int32, pad_shape, 1) == 0
    vt_ref[0, :, 0, A_V_DIM:A_VT_ROWS] = jnp.where(ones_row, 1.0, 0.0).astype(BF16)
    u_ref[...] = _dot(xn, w_ref[:, 2 * w + A_WIDTH:]).astype(BF16)


def _even_in(h2d, g, w_in, batch, tile):
    rows, d = h2d.shape
    n = w_in.shape[1]
    c = n // 4
    tiles = rows // batch // tile
    row_spec = pl.BlockSpec((tile, c), lambda i: (i, 0))
    return pl.pallas_call(
        _even_in_kernel,
        out_shape=(jax.ShapeDtypeStruct((2, rows, c), BF16),
                   jax.ShapeDtypeStruct((rows, c), BF16),
                   jax.ShapeDtypeStruct((batch, A_HEADS, tiles, A_VT_ROWS, tile), BF16),
                   jax.ShapeDtypeStruct((rows, c), BF16)),
        grid=(rows // tile,),
        in_specs=[pl.BlockSpec((tile, d), lambda i: (i, 0)), _resident((1, d)), _resident((d, n))],
        out_specs=(pl.BlockSpec((2, tile, c), lambda i: (0, i, 0)),
                   row_spec,
                   pl.BlockSpec((1, A_HEADS, 1, A_VT_ROWS, tile), lambda i: (i // tiles, 0, i % tiles, 0, 0)),
                   row_spec),
        compiler_params=_params(("parallel",),
                                _vmem_limit(tile * d * 4 + 5 * tile * c * 2, d * n * 2 + d * 4, 4 * tile * d * 4)),
        name="even_in",
    )(h2d, g, w_in)


def _attn_kernel(q_ref, k_ref, vt_ref, bias_ref, lq_ref, g_ref, o_ref, s_sc, m_sc, acc_sc,
                 *, tile, n_q_tiles, lambda_init):
    i = pl.program_id(2)
    qq = q_ref[:, 0].reshape(2 * tile, 2 * A_QK_DIM)
    m_sc[...] = jnp.full(m_sc.shape, -jnp.inf, F32)
    acc_sc[...] = jnp.zeros(acc_sc.shape, F32)

    n_chunks = 2 * tile // ATTN_QUERY_CHUNK
    chunk_cols = [slice(c * ATTN_QUERY_CHUNK, (c + 1) * ATTN_QUERY_CHUNK) for c in range(n_chunks)]

    def live_keys(c, diagonal):
        q_end = (c * ATTN_QUERY_CHUNK) % tile + ATTN_QUERY_CHUNK
        return q_end if diagonal else tile

    def produce(c, j, bias, dst, diagonal=False):
        keys = live_keys(c, diagonal)
        s = _dot_nt(k_ref[0, j * tile:j * tile + keys, :], qq[chunk_cols[c]])
        if bias is not None:
            b0 = (c * ATTN_QUERY_CHUNK) % tile
            s = s + bias[0:keys, b0:b0 + ATTN_QUERY_CHUNK]
        dst[0:keys, chunk_cols[c]] = s

    def consume(c, j, src, diagonal=False):
        keys = live_keys(c, diagonal)
        cols = chunk_cols[c]
        s = src[0:keys, cols]
        m_prev = m_sc[:, cols]
        m_new = jnp.maximum(m_prev, jnp.max(s, axis=0, keepdims=True))
        alpha = jnp.exp2(m_prev - m_new)
        p = jnp.exp2(s - m_new).astype(BF16)
        acc_sc[:, cols] = alpha * acc_sc[:, cols] + _dot(vt_ref[0, 0, j, :, 0:keys], p)
        m_sc[:, cols] = m_new

    def query_tile(qi):
        visits = [(qi, bias_ref.at[0, 0], True)]
        if qi >= 1:
            visits.append((qi - 1, bias_ref.at[0, 1], False))
        visits += [(j, None, False) for j in range(qi - 1)]
        for n in range(-1, len(visits)):
            for c in range(n_chunks):
                if n + 1 < len(visits):
                    j, bias, diagonal = visits[n + 1]
                    produce(c, j, bias, s_sc.at[(n + 1) % 2], diagonal)
                if n >= 0:
                    j, _, diagonal = visits[n]
                    consume(c, j, s_sc.at[n % 2], diagonal)

    for qi in range(n_q_tiles):
        pl.when(i == qi)(functools.partial(query_tile, qi))

    o = acc_sc[0:A_V_DIM] * (1.0 / acc_sc[A_V_DIM:A_V_DIM + 1])
    lq = lq_ref[...]
    lam = (jnp.exp(jnp.sum(lq[0:1] * lq[1:2], axis=-1, keepdims=True))
           - jnp.exp(jnp.sum(lq[2:3] * lq[3:4], axis=-1, keepdims=True)) + lambda_init)
    od = o[:, :tile] - lam * o[:, tile:]
    y = od * lax.rsqrt(jnp.mean(od * od, axis=0, keepdims=True) + EPS) * g_ref[...]
    o_ref[0] = (y * (1.0 - lambda_init)).T.astype(BF16)


def _diff_attention(q, k, vt, bias_tiles, lq, g_col, lambda_init):
    _, b, s, _ = q.shape
    tile = bias_tiles.shape[-1]
    tiles = s // tile
    hd = 2 * A_QK_DIM
    pipelined = (2 * tile * hd * 2 + s * hd * 2 + s * A_VT_ROWS * 2 + 2 * tile * tile * 4
                 + tile * A_V_DIM * 2)
    scratch = 2 * tile * 2 * tile * 4 + 8 * (2 * tile) * 4 + A_VT_ROWS * 2 * tile * 4
    return pl.pallas_call(
        functools.partial(_attn_kernel, tile=tile, n_q_tiles=tiles, lambda_init=lambda_init),
        out_shape=jax.ShapeDtypeStruct((b, s, A_WIDTH), BF16),
        grid=(b, A_HEADS, tiles),
        in_specs=[
            pl.BlockSpec((2, 1, tile, hd), lambda bi, h, i: (0, bi, i, h)),
            pl.BlockSpec((1, s, hd), lambda bi, h, i: (bi, 0, h)),
            pl.BlockSpec((1, 1, tiles, A_VT_ROWS, tile), lambda bi, h, i: (bi, h, 0, 0, 0)),
            pl.BlockSpec((1, 2, tile, tile), lambda bi, h, i: (h, 0, 0, 0)),
            _resident(lq.shape),
            _resident(g_col.shape),
        ],
        out_specs=pl.BlockSpec((1, tile, A_V_DIM), lambda bi, h, i: (bi, i, h)),
        scratch_shapes=[
            pltpu.VMEM((2, tile, 2 * tile), F32),
            pltpu.VMEM((1, 2 * tile), F32),
            pltpu.VMEM((A_VT_ROWS, 2 * tile), F32),
        ],
        compiler_params=_params(("parallel", "parallel", "parallel"),
                                _vmem_limit(pipelined, 128 * 128 * 4, scratch + 6 * tile * (2 * tile) * 4)),
        name="diff_attn",
    )(q, k, vt, bias_tiles, lq, g_col)


def _even_out_kernel(h_ref, o_ref, u_ref, uh_ref, pw_ref, ps_ref, w_ref, out_ref, *, tm):
    i = pl.program_id(1)
    u = u_ref[0].astype(F32)
    halo = jnp.where(i > 0, uh_ref[0].astype(F32), 0.0)
    ext = jnp.concatenate([halo, u], axis=0)
    t = i * tm + lax.broadcasted_iota(jnp.int32, (tm, 1), 0)
    ys = []
    for gi, w in enumerate(POOL_WINDOWS):
        cols = slice(gi * POOL_GROUP_DIM, (gi + 1) * POOL_GROUP_DIM)
        a = ext[:, cols]
        sh = 1
        while sh < w:
            a = a + pltpu.roll(a, sh, 0)
            sh *= 2
        inv_count = 1.0 / jnp.minimum(t + 1, w).astype(F32)
        p = a[POOL_HALO:] * inv_count - u[:, cols]
        ys.append((_dot(p.astype(BF16), pw_ref[gi]) * ps_ref[:, cols]).astype(BF16))
    mix = jnp.concatenate([o_ref[0]] + ys, axis=-1)
    out_ref[0] = h_ref[0] + _dot(mix, w_ref[...])


def _even_out(h, o, u, pool_w, pool_scale, w_out):
    b, s, d = h.shape
    tm = min(ROW_TILE, s)
    halo_blocks = tm // POOL_HALO
    return pl.pallas_call(
        functools.partial(_even_out_kernel, tm=tm),
        out_shape=jax.ShapeDtypeStruct(h.shape, F32),
        grid=(b, s // tm),
        in_specs=[
            pl.BlockSpec((1, tm, d), lambda bi, i: (bi, i, 0)),
            pl.BlockSpec((1, tm, A_WIDTH), lambda bi, i: (bi, i, 0)),
            pl.BlockSpec((1, tm, POOL_WIDTH), lambda bi, i: (bi, i, 0)),
            pl.BlockSpec((1, POOL_HALO, POOL_WIDTH),
                         lambda bi, i: (bi, jnp.maximum(i * halo_blocks - 1, 0), 0)),
            _resident(pool_w.shape),
            _resident(pool_scale.shape),
            _resident(w_out.shape),
        ],
        out_specs=pl.BlockSpec((1, tm, d), lambda bi, i: (bi, i, 0)),
        compiler_params=_params(("parallel", "parallel"),
                                _vmem_limit(2 * tm * d * 4 + 2 * tm * 512 * 2 + POOL_HALO * 512 * 2,
                                            (pool_w.size + w_out.size) * 2 + pool_scale.size * 4,
                                            4 * tm * d * 4)),
        name="even_out",
    )(h, o, u, u, pool_w, pool_scale, w_out)


def _odd_kernel(h_ref, g_ref, win_ref, cw_ref, wout_ref, out_ref, zprev_sc, *, tm):
    i = pl.program_id(1)
    h = h_ref[0]
    d = h.shape[-1]
    hn = _rms(h, g_ref[...]).astype(BF16)
    b_gate = _dot(hn, win_ref[:, 0:d])
    z = _dot(hn, win_ref[:, d:2 * d]) * _dot(hn, win_ref[:, 2 * d:3 * d])

    @pl.when(i == 0)
    def _():
        zprev_sc[...] = jnp.zeros(zprev_sc.shape, F32)

    ext = jnp.concatenate([zprev_sc[...], z], axis=0)
    zprev_sc[...] = z[tm - CONV_HALO:]
    cw = cw_ref[...]
    y = z * cw[CONV_WIDTH - 1:CONV_WIDTH]
    for tap in range(CONV_WIDTH - 1):
        back = CONV_WIDTH - 1 - tap
        y = y + pltpu.roll(ext, back, 0)[CONV_HALO:] * cw[tap:tap + 1]
    out_ref[0] = h + _dot((b_gate * y).astype(BF16), wout_ref[...])


def _odd_mixer(h, g, w_in, conv_w, w_out):
    b, s, d = h.shape
    tm = min(ROW_TILE, s)
    return pl.pallas_call(
        functools.partial(_odd_kernel, tm=tm),
        out_shape=jax.ShapeDtypeStruct(h.shape, F32),
        grid=(b, s // tm),
        in_specs=[
            pl.BlockSpec((1, tm, d), lambda bi, i: (bi, i, 0)),
            _resident(g.shape),
            _resident(w_in.shape),
            _resident(conv_w.shape),
            _resident(w_out.shape),
        ],
        out_specs=pl.BlockSpec((1, tm, d), lambda bi, i: (bi, i, 0)),
        scratch_shapes=[pltpu.VMEM((CONV_HALO, d), F32)],
        compiler_params=_params(("arbitrary", "arbitrary"),
                                _vmem_limit(2 * tm * d * 4, (w_in.size + w_out.size) * 2,
                                            8 * tm * d * 4)),
        name="odd_mixer",
    )(h, g, w_in, conv_w, w_out)


def _xattn_kernel(h_ref, g_ref, wq_ref, kv_ref, wo_ref, out_ref):
    h = h_ref[0]
    d = h.shape[-1]
    hd = d // X_HEADS
    hn = _rms(h, g_ref[...]).astype(BF16)
    q = (_dot(hn, wq_ref[...]) * (hd ** -0.5)).astype(BF16)
    outs = []
    for head in range(X_HEADS):
        cols = slice(head * hd, (head + 1) * hd)
        s = _dot_nt(q[:, cols], kv_ref[0, 0, :, cols])
        p = jnp.exp(s - jnp.max(s, axis=-1, keepdims=True))
        inv_l = 1.0 / jnp.sum(p, axis=-1, keepdims=True)
        v = kv_ref[0, 0, :, d + head * hd:d + (head + 1) * hd]
        outs.append((_dot(p.astype(BF16), v) * inv_l).astype(BF16))
    out_ref[0] = h + _dot(jnp.concatenate(outs, axis=-1), wo_ref[...])


def _xattn(h, g, wq, kv, layer, wo):
    b, s, d = h.shape
    n_mem = kv.shape[2]
    tm = min(ROW_TILE, s)
    return pl.pallas_call(
        _xattn_kernel,
        out_shape=jax.ShapeDtypeStruct(h.shape, F32),
        grid=(b, s // tm),
        in_specs=[
            pl.BlockSpec((1, tm, d), lambda bi, i: (bi, i, 0)),
            _resident(g.shape),
            _resident(wq.shape),
            pl.BlockSpec((1, 1, n_mem, 2 * d), lambda bi, i: (layer, bi, 0, 0)),
            _resident(wo.shape),
        ],
        out_specs=pl.BlockSpec((1, tm, d), lambda bi, i: (bi, i, 0)),
        compiler_params=_params(("parallel", "parallel"),
                                _vmem_limit(2 * tm * d * 4 + n_mem * 2 * d * 2,
                                            (wq.size + wo.size) * 2, 6 * tm * d * 4)),
        name="xattn",
    )(h, g, wq, kv, wo)


def _mlp_kernel(h_ref, g_ref, w1_ref, w2_ref, gf_ref, out_ref, *, chunk, final_norm):
    h = h_ref[...]
    hn = _rms(h, g_ref[...]).astype(BF16)
    acc = h
    for c in range(w1_ref.shape[1] // chunk):
        a = jnp.maximum(_dot(hn, w1_ref[:, c * chunk:(c + 1) * chunk]), 0.0)
        acc = acc + _dot((a * a).astype(BF16), w2_ref[c * chunk:(c + 1) * chunk, :])
    out_ref[...] = _rms(acc, gf_ref[...]) if final_norm else acc


def _mlp(h2d, g, w1, w2, gf, final_norm):
    rows, d = h2d.shape
    tm = min(ROW_TILE, rows)
    chunk = min(1024, w1.shape[1])
    return pl.pallas_call(
        functools.partial(_mlp_kernel, chunk=chunk, final_norm=final_norm),
        out_shape=jax.ShapeDtypeStruct(h2d.shape, F32),
        grid=(rows // tm,),
        in_specs=[
            pl.BlockSpec((tm, d), lambda i: (i, 0)),
            _resident(g.shape),
            _resident(w1.shape),
            _resident(w2.shape),
            _resident(gf.shape),
        ],
        out_specs=pl.BlockSpec((tm, d), lambda i: (i, 0)),
        compiler_params=_params(("parallel",),
                                _vmem_limit(2 * tm * d * 4, (w1.size + w2.size) * 2,
                                            2 * tm * d * 4 + 3 * tm * chunk * 4)),
        name="mlp",
    )(h2d, g, w1, w2, gf)


def _t5_bucket(n):
    small = n < REL_MAX_EXACT
    nf = jnp.maximum(n, 1).astype(F32)
    large = REL_MAX_EXACT + (jnp.log(nf / REL_MAX_EXACT) / math.log(REL_MAX_DIST / REL_MAX_EXACT)
                             * (REL_BUCKETS - REL_MAX_EXACT)).astype(jnp.int32)
    return jnp.where(small, n, jnp.minimum(large, REL_BUCKETS - 1))


def _bias_tiles(rel_bias, tile):
    assert tile + 1 >= REL_MAX_DIST
    rb = rel_bias.astype(F32)
    rb = (rb - rb[REL_BUCKETS - 1]).T[:, :, None, None]
    pos = jnp.arange(tile, dtype=jnp.int32)
    out = []
    for delta in (0, 1):
        dist = delta * tile + pos[None, :] - pos[:, None]
        bucket = _t5_bucket(jnp.maximum(dist, 0))
        vals = jnp.zeros((rb.shape[0], tile, tile), F32)
        for bkt in range(REL_BUCKETS - 1):
            vals = jnp.where(bucket == bkt, rb[:, bkt], vals)
        out.append(jnp.where(dist >= 0, vals * LOG2E, NEG_BIG))
    return jnp.stack(out, axis=1)


def kernel(x, mem, rel_bias, mem_norm_g, norm_mix_g, norm_xattn_g, norm_mlp_g, final_norm_g,
           ab_w_in, ab_w_out, lambda_q1, lambda_k1, lambda_q2, lambda_k2, subln_g, pool_w,
           pool_scale, conv_w_in, conv_w, conv_w_out, xattn_wq, xattn_wkv, xattn_wo, mlp_w1, mlp_w2):
    b, s, d = x.shape
    depth = norm_mix_g.shape[0]
    rows = b * s
    tile = min(ATTN_TILE, s)

    ab_w_in, ab_w_out, pool_w, conv_w_in, conv_w_out, xattn_wq, xattn_wkv, xattn_wo, mlp_w1, mlp_w2 = (
        w.astype(BF16) for w in (ab_w_in, ab_w_out, pool_w, conv_w_in, conv_w_out, xattn_wq,
                                 xattn_wkv, xattn_wo, mlp_w1, mlp_w2))

    bias_tiles = _bias_tiles(rel_bias, tile)
    kv = _memkv(mem.reshape(b * mem.shape[1], d), mem_norm_g.reshape(1, d), xattn_wkv)
    kv = kv.reshape(depth, b, mem.shape[1], 2 * d)

    h = x
    for l in range(depth):
        i = l // 2
        g_mix = norm_mix_g[l].reshape(1, d)
        if l % 2 == 0:
            lambda_init = 0.8 - 0.6 * math.exp(-0.3 * l)
            q, k, vt, u = _even_in(h.reshape(rows, d), g_mix, ab_w_in[i], b, tile)
            lq = jnp.stack([lambda_q1[i], lambda_k1[i], lambda_q2[i], lambda_k2[i]]).astype(F32)
            o = _diff_attention(q.reshape(2, b, s, -1), k.reshape(b, s, -1), vt,
                                bias_tiles, lq, subln_g[i].reshape(-1, 1), lambda_init)
            h = _even_out(h, o, u.reshape(b, s, -1), pool_w[i], pool_scale[i].reshape(1, -1), ab_w_out[i])
        else:
            h = _odd_mixer(h, g_mix, conv_w_in[i], conv_w[i], conv_w_out[i])
        h = _xattn(h, norm_xattn_g[l].reshape(1, d), xattn_wq[l], kv, l, xattn_wo[l])
        h = _mlp(h.reshape(rows, d), norm_mlp_g[l].reshape(1, d), mlp_w1[l], mlp_w2[l],
                 final_norm_g.reshape(1, d), l == depth - 1).reshape(b, s, d)
    return h
```

```python
import functools
import math

import jax
import jax.numpy as jnp
from jax import lax
from jax.experimental import pallas as pl
from jax.experimental.pallas import tpu as pltpu

F32 = jnp.float32
BF16 = jnp.bfloat16

EPS = 1e-6
A_HEADS = 4
A_QK_DIM = 64
A_V_DIM = 2 * A_QK_DIM
A_WIDTH = A_HEADS * A_V_DIM
A_VT_ROWS = A_V_DIM + 16
POOL_WINDOWS = (2, 4, 8, 16)
POOL_GROUP_DIM = 128
POOL_WIDTH = len(POOL_WINDOWS) * POOL_GROUP_DIM
POOL_HALO = 16
CONV_WIDTH = 3
CONV_HALO = 8
X_HEADS = 4
REL_BUCKETS = 32
REL_MAX_EXACT = REL_BUCKETS // 2
REL_MAX_DIST = 128

V7X_VMEM_LIMIT_CAP = 56 * 1024 * 1024

ROW_TILE = 1024
ATTN_TILE = 512
ATTN_QUERY_CHUNK = 256
LOG2E = math.log2(math.e)
NEG_BIG = -1e30


def _vmem_limit(pipelined_bytes, resident_bytes, live_bytes):
    return min(2 * pipelined_bytes + resident_bytes + live_bytes, V7X_VMEM_LIMIT_CAP)


def _params(semantics, vmem):
    return pltpu.CompilerParams(dimension_semantics=semantics, vmem_limit_bytes=vmem)


def _resident(shape):
    return pl.BlockSpec(shape, lambda *_: (0,) * len(shape), pipeline_mode=pl.Buffered(1))


def _rms(x, g):
    return x * lax.rsqrt(jnp.mean(x * x, axis=-1, keepdims=True) + EPS) * g


def _dot(a, b):
    return jnp.dot(a, b, preferred_element_type=F32)


def _dot_nt(a, b):
    return lax.dot_general(a, b, (((1,), (1,)), ((), ())), preferred_element_type=F32)


def _memkv_kernel(x_ref, g_ref, w_ref, o_ref):
    xn = _rms(x_ref[...], g_ref[...]).astype(BF16)
    o_ref[0] = _dot(xn, w_ref[0]).astype(BF16)


def _memkv(mem2d, g, wkv):
    rows, d = mem2d.shape
    depth, _, n = wkv.shape
    tm = min(ROW_TILE, rows)
    return pl.pallas_call(
        _memkv_kernel,
        out_shape=jax.ShapeDtypeStruct((depth, rows, n), BF16),
        grid=(depth, rows // tm),
        in_specs=[
            pl.BlockSpec((tm, d), lambda l, i: (i, 0)),
            _resident((1, d)),
            pl.BlockSpec((1, d, n), lambda l, i: (l, 0, 0)),
        ],
        out_specs=pl.BlockSpec((1, tm, n), lambda l, i: (l, i, 0)),
        compiler_params=_params(("parallel", "parallel"),
                                _vmem_limit(tm * d * 4 + d * n * 2 + tm * n * 2, d * 4, 2 * tm * n * 4)),
        name="mem_kv",
    )(mem2d, g, wkv)


def _even_in_kernel(x_ref, g_ref, w_ref, q_ref, k_ref, vt_ref, u_ref):
    xn = _rms(x_ref[...], g_ref[...]).astype(BF16)
    w = A_HEADS * 2 * A_QK_DIM
    q = _dot(xn, w_ref[:, 0:w]) * (A_QK_DIM ** -0.5 * LOG2E)
    first = (lax.broadcasted_iota(jnp.int32, q.shape, 1) % (2 * A_QK_DIM)) < A_QK_DIM
    q_ref[0] = jnp.where(first, q, 0.0).astype(BF16)
    q_ref[1] = jnp.where(first, 0.0, q).astype(BF16)
    k_ref[...] = _dot(xn, w_ref[:, w:2 * w]).astype(BF16)
    vt = _dot(xn, w_ref[:, 2 * w:2 * w + A_WIDTH]).T.astype(BF16)
    vt_ref[0, :, 0, 0:A_V_DIM] = vt.reshape(A_HEADS, A_V_DIM, vt.shape[-1])
    pad_shape = (A_HEADS, A_VT_ROWS - A_V_DIM, vt.shape[-1])
    ones_row = lax.broadcasted_iota(jnp.int32, pad_shape, 1) == 0
    vt_ref[0, :, 0, A_V_DIM:A_VT_ROWS] = jnp.where(ones_row, 1.0, 0.0).astype(BF16)
    u_ref[...] = _dot(xn, w_ref[:, 2 * w + A_WIDTH:]).astype(BF16)


def _even_in(h2d, g, w_in, batch, tile):
    rows, d = h2d.shape
    n = w_in.shape[1]
    c = n // 4
    tiles = rows // batch // tile
    row_spec = pl.BlockSpec((tile, c), lambda i: (i, 0))
    return pl.pallas_call(
        _even_in_kernel,
        out_shape=(jax.ShapeDtypeStruct((2, rows, c), BF16),
                   jax.ShapeDtypeStruct((rows, c), BF16),
                   jax.ShapeDtypeStruct((batch, A_HEADS, tiles, A_VT_ROWS, tile), BF16),
                   jax.ShapeDtypeStruct((rows, c), BF16)),
        grid=(rows // tile,),
        in_specs=[pl.BlockSpec((tile, d), lambda i: (i, 0)), _resident((1, d)), _resident((d, n))],
        out_specs=(pl.BlockSpec((2, tile, c), lambda i: (0, i, 0)),
                   row_spec,
                   pl.BlockSpec((1, A_HEADS, 1, A_VT_ROWS, tile), lambda i: (i // tiles, 0, i % tiles, 0, 0)),
                   row_spec),
        compiler_params=_params(("parallel",),
                                _vmem_limit(tile * d * 4 + 5 * tile * c * 2, d * n * 2 + d * 4, 4 * tile * d * 4)),
        name="even_in",
    )(h2d, g, w_in)


def _attn_kernel(q_ref, k_ref, vt_ref, bias_ref, lq_ref, g_ref, o_ref, s_sc, m_sc, acc_sc,
                 *, tile, n_q_tiles, lambda_init):
    i = pl.program_id(2)
    qq = q_ref[:, 0].reshape(2 * tile, 2 * A_QK_DIM)
    m_sc[...] = jnp.full(m_sc.shape, -jnp.inf, F32)
    acc_sc[...] = jnp.zeros(acc_sc.shape, F32)

    n_chunks = 2 * tile // ATTN_QUERY_CHUNK
    chunk_cols = [slice(c * ATTN_QUERY_CHUNK, (c + 1) * ATTN_QUERY_CHUNK) for c in range(n_chunks)]

    def live_keys(c, diagonal):
        q_end = (c * ATTN_QUERY_CHUNK) % tile + ATTN_QUERY_CHUNK
        return q_end if diagonal else tile

    def produce(c, j, bias, dst, diagonal=False):
        keys = live_keys(c, diagonal)
        s = _dot_nt(k_ref[0, j * tile:j * tile + keys, :], qq[chunk_cols[c]])
        if bias is not None:
            b0 = (c * ATTN_QUERY_CHUNK) % tile
            s = s + bias[0:keys, b0:b0 + ATTN_QUERY_CHUNK]
        dst[0:keys, chunk_cols[c]] = s

    def consume(c, j, src, diagonal=False):
        keys = live_keys(c, diagonal)
        cols = chunk_cols[c]
        s = src[0:keys, cols]
        m_prev = m_sc[:, cols]
        m_new = jnp.maximum(m_prev, jnp.max(s, axis=0, keepdims=True))
        alpha = jnp.exp2(m_prev - m_new)
        p = jnp.exp2(s - m_new).astype(BF16)
        acc_sc[:, cols] = alpha * acc_sc[:, cols] + _dot(vt_ref[0, 0, j, :, 0:keys], p)
        m_sc[:, cols] = m_new

    def query_tile(qi):
        visits = [(qi, bias_ref.at[0, 0], True)]
        if qi >= 1:
            visits.append((qi - 1, bias_ref.at[0, 1], False))
        visits += [(j, None, False) for j in range(qi - 1)]
        for n in range(-1, len(visits)):
            for c in range(n_chunks):
                if n + 1 < len(visits):
                    j, bias, diagonal = visits[n + 1]
                    produce(c, j, bias, s_sc.at[(n + 1) % 2], diagonal)
                if n >= 0:
                    j, _, diagonal = visits[n]
                    consume(c, j, s_sc.at[n % 2], diagonal)

    for qi in range(n_q_tiles):
        pl.when(i == qi)(functools.partial(query_tile, qi))

    o = acc_sc[0:A_V_DIM] * (1.0 / acc_sc[A_V_DIM:A_V_DIM + 1])
    lq = lq_ref[...]
    lam = (jnp.exp(jnp.sum(lq[0:1] * lq[1:2], axis=-1, keepdims=True))
           - jnp.exp(jnp.sum(lq[2:3] * lq[3:4], axis=-1, keepdims=True)) + lambda_init)
    od = o[:, :tile] - lam * o[:, tile:]
    y = od * lax.rsqrt(jnp.mean(od * od, axis=0, keepdims=True) + EPS) * g_ref[...]
    o_ref[0] = (y * (1.0 - lambda_init)).T.astype(BF16)


def _diff_attention(q, k, vt, bias_tiles, lq, g_col, lambda_init):
    _, b, s, _ = q.shape
    tile = bias_tiles.shape[-1]
    tiles = s // tile
    hd = 2 * A_QK_DIM
    pipelined = (2 * tile * hd * 2 + s * hd * 2 + s * A_VT_ROWS * 2 + 2 * tile * tile * 4
                 + tile * A_V_DIM * 2)
    scratch = 2 * tile * 2 * tile * 4 + 8 * (2 * tile) * 4 + A_VT_ROWS * 2 * tile * 4
    return pl.pallas_call(
        functools.partial(_attn_kernel, tile=tile, n_q_tiles=tiles, lambda_init=lambda_init),
        out_shape=jax.ShapeDtypeStruct((b, s, A_WIDTH), BF16),
        grid=(b, A_HEADS, tiles),
        in_specs=[
            pl.BlockSpec((2, 1, tile, hd), lambda bi, h, i: (0, bi, i, h)),
            pl.BlockSpec((1, s, hd), lambda bi, h, i: (bi, 0, h)),
            pl.BlockSpec((1, 1, tiles, A_VT_ROWS, tile), lambda bi, h, i: (bi, h, 0, 0, 0)),
            pl.BlockSpec((1, 2, tile, tile), lambda bi, h, i: (h, 0, 0, 0)),
            _resident(lq.shape),
            _resident(g_col.shape),
        ],
        out_specs=pl.BlockSpec((1, tile, A_V_DIM), lambda bi, h, i: (bi, i, h)),
        scratch_shapes=[
            pltpu.VMEM((2, tile, 2 * tile), F32),
            pltpu.VMEM((1, 2 * tile), F32),
            pltpu.VMEM((A_VT_ROWS, 2 * tile), F32),
        ],
        compiler_params=_params(("parallel", "parallel", "parallel"),
                                _vmem_limit(pipelined, 128 * 128 * 4, scratch + 6 * tile * (2 * tile) * 4)),
        name="diff_attn",
    )(q, k, vt, bias_tiles, lq, g_col)


def _even_out_kernel(h_ref, o_ref, u_ref, uh_ref, pw_ref, ps_ref, w_ref, out_ref, *, tm):
    i = pl.program_id(1)
    u = u_ref[0].astype(F32)
    halo = jnp.where(i > 0, uh_ref[0].astype(F32), 0.0)
    ext = jnp.concatenate([halo, u], axis=0)
    t = i * tm + lax.broadcasted_iota(jnp.int32, (tm, 1), 0)
    ys = []
    for gi, w in enumerate(POOL_WINDOWS):
        cols = slice(gi * POOL_GROUP_DIM, (gi + 1) * POOL_GROUP_DIM)
        a = ext[:, cols]
        sh = 1
        while sh < w:
            a = a + pltpu.roll(a, sh, 0)
            sh *= 2
        inv_count = 1.0 / jnp.minimum(t + 1, w).astype(F32)
        p = a[POOL_HALO:] * inv_count - u[:, cols]
        ys.append((_dot(p.astype(BF16), pw_ref[gi]) * ps_ref[:, cols]).astype(BF16))
    mix = jnp.concatenate([o_ref[0]] + ys, axis=-1)
    out_ref[0] = h_ref[0] + _dot(mix, w_ref[...])


def _even_out(h, o, u, pool_w, pool_scale, w_out):
    b, s, d = h.shape
    tm = min(ROW_TILE, s)
    halo_blocks = tm // POOL_HALO
    return pl.pallas_call(
        functools.partial(_even_out_kernel, tm=tm),
        out_shape=jax.ShapeDtypeStruct(h.shape, F32),
        grid=(b, s // tm),
        in_specs=[
            pl.BlockSpec((1, tm, d), lambda bi, i: (bi, i, 0)),
            pl.BlockSpec((1, tm, A_WIDTH), lambda bi, i: (bi, i, 0)),
            pl.BlockSpec((1, tm, POOL_WIDTH), lambda bi, i: (bi, i, 0)),
            pl.BlockSpec((1, POOL_HALO, POOL_WIDTH),
                         lambda bi, i: (bi, jnp.maximum(i * halo_blocks - 1, 0), 0)),
            _resident(pool_w.shape),
            _resident(pool_scale.shape),
            _resident(w_out.shape),
        ],
        out_specs=pl.BlockSpec((1, tm, d), lambda bi, i: (bi, i, 0)),
        compiler_params=_params(("parallel", "parallel"),
                                _vmem_limit(2 * tm * d * 4 + 2 * tm * 512 * 2 + POOL_HALO * 512 * 2,
                                            (pool_w.size + w_out.size) * 2 + pool_scale.size * 4,
                                            4 * tm * d * 4)),
        name="even_out",
    )(h, o, u, u, pool_w, pool_scale, w_out)


def _odd_kernel(h_ref, g_ref, win_ref, cw_ref, wout_ref, out_ref, zprev_sc, *, tm):
    i = pl.program_id(1)
    h = h_ref[0]
    d = h.shape[-1]
    hn = _rms(h, g_ref[...]).astype(BF16)
    b_gate = _dot(hn, win_ref[:, 0:d])
    z = _dot(hn, win_ref[:, d:2 * d]) * _dot(hn, win_ref[:, 2 * d:3 * d])

    @pl.when(i == 0)
    def _():
        zprev_sc[...] = jnp.zeros(zprev_sc.shape, F32)

    ext = jnp.concatenate([zprev_sc[...], z], axis=0)
    zprev_sc[...] = z[tm - CONV_HALO:]
    cw = cw_ref[...]
    y = z * cw[CONV_WIDTH - 1:CONV_WIDTH]
    for tap in range(CONV_WIDTH - 1):
        back = CONV_WIDTH - 1 - tap
        y = y + pltpu.roll(ext, back, 0)[CONV_HALO:] * cw[tap:tap + 1]
    out_ref[0] = h + _dot((b_gate * y).astype(BF16), wout_ref[...])


def _odd_mixer(h, g, w_in, conv_w, w_out):
    b, s, d = h.shape
    tm = min(ROW_TILE, s)
    return pl.pallas_call(
        functools.partial(_odd_kernel, tm=tm),
        out_shape=jax.ShapeDtypeStruct(h.shape, F32),
        grid=(b, s // tm),
        in_specs=[
            pl.BlockSpec((1, tm, d), lambda bi, i: (bi, i, 0)),
            _resident(g.shape),
            _resident(w_in.shape),
            _resident(conv_w.shape),
            _resident(w_out.shape),
        ],
        out_specs=pl.BlockSpec((1, tm, d), lambda bi, i: (bi, i, 0)),
        scratch_shapes=[pltpu.VMEM((CONV_HALO, d), F32)],
        compiler_params=_params(("arbitrary", "arbitrary"),
                                _vmem_limit(2 * tm * d * 4, (w_in.size + w_out.size) * 2,
                                            8 * tm * d * 4)),
        name="odd_mixer",
    )(h, g, w_in, conv_w, w_out)


def _xattn_kernel(h_ref, g_ref, wq_ref, kv_ref, wo_ref, out_ref):
    h = h_ref[0]
    d = h.shape[-1]
    hd = d // X_HEADS
    hn = _rms(h, g_ref[...]).astype(BF16)
    q = (_dot(hn, wq_ref[...]) * (hd ** -0.5)).astype(BF16)
    outs = []
    for head in range(X_HEADS):
        cols = slice(head * hd, (head + 1) * hd)
        s = _dot_nt(q[:, cols], kv_ref[0, 0, :, cols])
        p = jnp.exp(s - jnp.max(s, axis=-1, keepdims=True))
        inv_l = 1.0 / jnp.sum(p, axis=-1, keepdims=True)
        v = kv_ref[0, 0, :, d + head * hd:d + (head + 1) * hd]
        outs.append((_dot(p.astype(BF16), v) * inv_l).astype(BF16))
    out_ref[0] = h + _dot(jnp.concatenate(outs, axis=-1), wo_ref[...])


def _xattn(h, g, wq, kv, layer, wo):
    b, s, d = h.shape
    n_mem = kv.shape[2]
    tm = min(ROW_TILE, s)
    return pl.pallas_call(
        _xattn_kernel,
        out_shape=jax.ShapeDtypeStruct(h.shape, F32),
        grid=(b, s // tm),
        in_specs=[
            pl.BlockSpec((1, tm, d), lambda bi, i: (bi, i, 0)),
            _resident(g.shape),
            _resident(wq.shape),
            pl.BlockSpec((1, 1, n_mem, 2 * d), lambda bi, i: (layer, bi, 0, 0)),
            _resident(wo.shape),
        ],
        out_specs=pl.BlockSpec((1, tm, d), lambda bi, i: (bi, i, 0)),
        compiler_params=_params(("parallel", "parallel"),
                                _vmem_limit(2 * tm * d * 4 + n_mem * 2 * d * 2,
                                            (wq.size + wo.size) * 2, 6 * tm * d * 4)),
        name="xattn",
    )(h, g, wq, kv, wo)


def _mlp_kernel(h_ref, g_ref, w1_ref, w2_ref, gf_ref, out_ref, *, chunk, final_norm):
    h = h_ref[...]
    hn = _rms(h, g_ref[...]).astype(BF16)
    acc = h
    for c in range(w1_ref.shape[1] // chunk):
        a = jnp.maximum(_dot(hn, w1_ref[:, c * chunk:(c + 1) * chunk]), 0.0)
        acc = acc + _dot((a * a).astype(BF16), w2_ref[c * chunk:(c + 1) * chunk, :])
    out_ref[...] = _rms(acc, gf_ref[...]) if final_norm else acc


def _mlp(h2d, g, w1, w2, gf, final_norm):
    rows, d = h2d.shape
    tm = min(ROW_TILE, rows)
    chunk = min(1024, w1.shape[1])
    return pl.pallas_call(
        functools.partial(_mlp_kernel, chunk=chunk, final_norm=final_norm),
        out_shape=jax.ShapeDtypeStruct(h2d.shape, F32),
        grid=(rows // tm,),
        in_specs=[
            pl.BlockSpec((tm, d), lambda i: (i, 0)),
            _resident(g.shape),
            _resident(w1.shape),
            _resident(w2.shape),
            _resident(gf.shape),
        ],
        out_specs=pl.BlockSpec((tm, d), lambda i: (i, 0)),
        compiler_params=_params(("parallel",),
                                _vmem_limit(2 * tm * d * 4, (w1.size + w2.size) * 2,
                                            2 * tm * d * 4 + 3 * tm * chunk * 4)),
        name="mlp",
    )(h2d, g, w1, w2, gf)


def _t5_bucket(n):
    small = n < REL_MAX_EXACT
    nf = jnp.maximum(n, 1).astype(F32)
    large = REL_MAX_EXACT + (jnp.log(nf / REL_MAX_EXACT) / math.log(REL_MAX_DIST / REL_MAX_EXACT)
                             * (REL_BUCKETS - REL_MAX_EXACT)).astype(jnp.int32)
    return jnp.where(small, n, jnp.minimum(large, REL_BUCKETS - 1))


def _bias_tiles(rel_bias, tile):
    assert tile + 1 >= REL_MAX_DIST
    rb = rel_bias.astype(F32)
    rb = (rb - rb[REL_BUCKETS - 1]).T[:, :, None, None]
    pos = jnp.arange(tile, dtype=jnp.int32)
    out = []
    for delta in (0, 1):
        dist = delta * tile + pos[None, :] - pos[:, None]
        bucket = _t5_bucket(jnp.maximum(dist, 0))
        vals = jnp.zeros((rb.shape[0], tile, tile), F32)
        for bkt in range(REL_BUCKETS - 1):
            vals = jnp.where(bucket == bkt, rb[:, bkt], vals)
        out.append(jnp.where(dist >= 0, vals * LOG2E, NEG_BIG))
    return jnp.stack(out, axis=1)


def kernel(x, mem, rel_bias, mem_norm_g, norm_mix_g, norm_xattn_g, norm_mlp_g, final_norm_g,
           ab_w_in, ab_w_out, lambda_q1, lambda_k1, lambda_q2, lambda_k2, subln_g, pool_w,
           pool_scale, conv_w_in, conv_w, conv_w_out, xattn_wq, xattn_wkv, xattn_wo, mlp_w1, mlp_w2):
    b, s, d = x.shape
    depth = norm_mix_g.shape[0]
    rows = b * s
    tile = min(ATTN_TILE, s)

    ab_w_in, ab_w_out, pool_w, conv_w_in, conv_w_out, xattn_wq, xattn_wkv, xattn_wo, mlp_w1, mlp_w2 = (
        w.astype(BF16) for w in (ab_w_in, ab_w_out, pool_w, conv_w_in, conv_w_out, xattn_wq,
                                 xattn_wkv, xattn_wo, mlp_w1, mlp_w2))

    bias_tiles = _bias_tiles(rel_bias, tile)
    kv = _memkv(mem.reshape(b * mem.shape[1], d), mem_norm_g.reshape(1, d), xattn_wkv)
    kv = kv.reshape(depth, b, mem.shape[1], 2 * d)

    h = x
    for l in range(depth):
        i = l // 2
        g_mix = norm_mix_g[l].reshape(1, d)
        if l % 2 == 0:
            lambda_init = 0.8 - 0.6 * math.exp(-0.3 * l)
            q, k, vt, u = _even_in(h.reshape(rows, d), g_mix, ab_w_in[i], b, tile)
            lq = jnp.stack([lambda_q1[i], lambda_k1[i], lambda_q2[i], lambda_k2[i]]).astype(F32)
            o = _diff_attention(q.reshape(2, b, s, -1), k.reshape(b, s, -1), vt,
                                bias_tiles, lq, subln_g[i].reshape(-1, 1), lambda_init)
            h = _even_out(h, o, u.reshape(b, s, -1), pool_w[i], pool_scale[i].reshape(1, -1), ab_w_out[i])
        else:
            h = _odd_mixer(h, g_mix, conv_w_in[i], conv_w[i], conv_w_out[i])
        h = _xattn(h, norm_xattn_g[l].reshape(1, d), xattn_wq[l], kv, l, xattn_wo[l])
        h = _mlp(h.reshape(rows, d), norm_mlp_g[l].reshape(1, d), mlp_w1[l], mlp_w2[l],
                 final_norm_g.reshape(1, d), l == depth - 1).reshape(b, s, d)
    return h
```

```python
import functools
import math

import jax
import jax.numpy as jnp
from jax import lax
from jax.experimental import pallas as pl
from jax.experimental.pallas import tpu as pltpu

F32 = jnp.float32
BF16 = jnp.bfloat16

EPS = 1e-6
A_HEADS = 4
A_QK_DIM = 64
A_V_DIM = 2 * A_QK_DIM
A_WIDTH = A_HEADS * A_V_DIM
A_VT_ROWS = A_V_DIM + 16
POOL_WINDOWS = (2, 4, 8, 16)
POOL_GROUP_DIM = 128
POOL_WIDTH = len(POOL_WINDOWS) * POOL_GROUP_DIM
POOL_HALO = 16
CONV_WIDTH = 3
CONV_HALO = 8
X_HEADS = 4
REL_BUCKETS = 32
REL_MAX_EXACT = REL_BUCKETS // 2
REL_MAX_DIST = 128

V7X_VMEM_LIMIT_CAP = 56 * 1024 * 1024

ROW_TILE = 1024
ATTN_TILE = 512
ATTN_QUERY_CHUNK = 256
LOG2E = math.log2(math.e)
NEG_BIG = -1e30


def _vmem_limit(pipelined_bytes, resident_bytes, live_bytes):
    return min(2 * pipelined_bytes + resident_bytes + live_bytes, V7X_VMEM_LIMIT_CAP)


def _params(semantics, vmem):
    return pltpu.CompilerParams(dimension_semantics=semantics, vmem_limit_bytes=vmem)


def _resident(shape):
    return pl.BlockSpec(shape, lambda *_: (0,) * len(shape), pipeline_mode=pl.Buffered(1))


def _rms(x, g):
    return x * lax.rsqrt(jnp.mean(x * x, axis=-1, keepdims=True) + EPS) * g


def _dot(a, b):
    return jnp.dot(a, b, preferred_element_type=F32)


def _dot_nt(a, b):
    return lax.dot_general(a, b, (((1,), (1,)), ((), ())), preferred_element_type=F32)


def _memkv_kernel(x_ref, g_ref, w_ref, o_ref):
    xn = _rms(x_ref[...], g_ref[...]).astype(BF16)
    o_ref[0] = _dot(xn, w_ref[0]).astype(BF16)


def _memkv(mem2d, g, wkv):
    rows, d = mem2d.shape
    depth, _, n = wkv.shape
    tm = min(ROW_TILE, rows)
    return pl.pallas_call(
        _memkv_kernel,
        out_shape=jax.ShapeDtypeStruct((depth, rows, n), BF16),
        grid=(depth, rows // tm),
        in_specs=[
            pl.BlockSpec((tm, d), lambda l, i: (i, 0)),
            _resident((1, d)),
            pl.BlockSpec((1, d, n), lambda l, i: (l, 0, 0)),
        ],
        out_specs=pl.BlockSpec((1, tm, n), lambda l, i: (l, i, 0)),
        compiler_params=_params(("parallel", "parallel"),
                                _vmem_limit(tm * d * 4 + d * n * 2 + tm * n * 2, d * 4, 2 * tm * n * 4)),
        name="mem_kv",
    )(mem2d, g, wkv)


def _even_in_kernel(x_ref, g_ref, w_ref, q_ref, k_ref, vt_ref, u_ref):
    xn = _rms(x_ref[...], g_ref[...]).astype(BF16)
    w = A_HEADS * 2 * A_QK_DIM
    q = _dot(xn, w_ref[:, 0:w]) * (A_QK_DIM ** -0.5 * LOG2E)
    first = (lax.broadcasted_iota(jnp.int32, q.shape, 1) % (2 * A_QK_DIM)) < A_QK_DIM
    q_ref[0] = jnp.where(first, q, 0.0).astype(BF16)
    q_ref[1] = jnp.where(first, 0.0, q).astype(BF16)
    k_ref[...] = _dot(xn, w_ref[:, w:2 * w]).astype(BF16)
    vt = _dot(xn, w_ref[:, 2 * w:2 * w + A_WIDTH]).T.astype(BF16)
    vt_ref[0, :, 0, 0:A_V_DIM] = vt.reshape(A_HEADS, A_V_DIM, vt.shape[-1])
    pad_shape = (A_HEADS, A_VT_ROWS - A_V_DIM, vt.shape[-1])
    ones_row = lax.broadcasted_iota(jnp.int32, pad_shape, 1) == 0
    vt_ref[0, :, 0, A_V_DIM:A_VT_ROWS] = jnp.where(ones_row, 1.0, 0.0).astype(BF16)
    u_ref[...] = _dot(xn, w_ref[:, 2 * w + A_WIDTH:]).astype(BF16)


def _even_in(h2d, g, w_in, batch, tile):
    rows, d = h2d.shape
    n = w_in.shape[1]
    c = n // 4
    tiles = rows // batch // tile
    row_spec = pl.BlockSpec((tile, c), lambda i: (i, 0))
    return pl.pallas_call(
        _even_in_kernel,
        out_shape=(jax.ShapeDtypeStruct((2, rows, c), BF16),
                   jax.ShapeDtypeStruct((rows, c), BF16),
                   jax.ShapeDtypeStruct((batch, A_HEADS, tiles, A_VT_ROWS, tile), BF16),
                   jax.ShapeDtypeStruct((rows, c), BF16)),
        grid=(rows // tile,),
        in_specs=[pl.BlockSpec((tile, d), lambda i: (i, 0)), _resident((1, d)), _resident((d, n))],
        out_specs=(pl.BlockSpec((2, tile, c), lambda i: (0, i, 0)),
                   row_spec,
                   pl.BlockSpec((1, A_HEADS, 1, A_VT_ROWS, tile), lambda i: (i // tiles, 0, i % tiles, 0, 0)),
                   row_spec),
        compiler_params=_params(("parallel",),
                                _vmem_limit(tile * d * 4 + 5 * tile * c * 2, d * n * 2 + d * 4, 4 * tile * d * 4)),
        name="even_in",
    )(h2d, g, w_in)


def _attn_kernel(q_ref, k_ref, vt_ref, bias_ref, lq_ref, g_ref, o_ref, s_sc, m_sc, acc_sc,
                 *, tile, n_q_tiles, lambda_init):
    n_chunks = 2 * tile // ATTN_QUERY_CHUNK
    chunk_cols = [slice(c * ATTN_QUERY_CHUNK, (c + 1) * ATTN_QUERY_CHUNK) for c in range(n_chunks)]

    def live_keys(c, diagonal):
        q_end = (c * ATTN_QUERY_CHUNK) % tile + ATTN_QUERY_CHUNK
        return q_end if diagonal else tile

    def produce(c, qi, j, bias, buf, diagonal):
        keys = live_keys(c, diagonal)
        q0 = qi * tile + (c * ATTN_QUERY_CHUNK) % tile
        qc = q_ref[(c * ATTN_QUERY_CHUNK) // tile, 0, q0:q0 + ATTN_QUERY_CHUNK, :]
        s = _dot_nt(k_ref[0, j * tile:j * tile + keys, :], qc)
        if bias is not None:
            b0 = (c * ATTN_QUERY_CHUNK) % tile
            s = s + bias[0:keys, b0:b0 + ATTN_QUERY_CHUNK]
        s_sc[buf, 0:keys, chunk_cols[c]] = s

    def consume(c, j, buf, m_ref, acc_ref, diagonal):
        keys = live_keys(c, diagonal)
        cols = chunk_cols[c]
        s = s_sc[buf, 0:keys, cols]
        m_prev = m_ref[:, cols]
        m_new = jnp.maximum(m_prev, jnp.max(s, axis=0, keepdims=True))
        alpha = jnp.exp2(m_prev - m_new)
        p = jnp.exp2(s - m_new).astype(BF16)
        acc_ref[:, cols] = alpha * acc_ref[:, cols] + _dot(vt_ref[0, 0, j, :, 0:keys], p)
        m_ref[:, cols] = m_new

    def finalize(qi, acc_ref):
        o = acc_ref[0:A_V_DIM] * (1.0 / acc_ref[A_V_DIM:A_V_DIM + 1])
        lq = lq_ref[...]
        lam = (jnp.exp(jnp.sum(lq[0:1] * lq[1:2], axis=-1, keepdims=True))
               - jnp.exp(jnp.sum(lq[2:3] * lq[3:4], axis=-1, keepdims=True)) + lambda_init)
        od = o[:, :tile] - lam * o[:, tile:]
        y = od * lax.rsqrt(jnp.mean(od * od, axis=0, keepdims=True) + EPS) * g_ref[...]
        o_ref[0, qi * tile:(qi + 1) * tile, :] = (y * (1.0 - lambda_init)).T.astype(BF16)

    def run(q_tiles):
        visits = []
        for slot, qi in enumerate(q_tiles):
            tiles = [(qi, bias_ref.at[0, 0], True)]
            if qi >= 1:
                tiles.append((qi - 1, bias_ref.at[0, 1], False))
            tiles += [(j, None, False) for j in range(qi - 1)]
            visits += [(qi, slot % 2, n == 0, n == len(tiles) - 1) + t for n, t in enumerate(tiles)]

        for n in range(-1, len(visits)):
            if n >= 0:
                qi, slot, first, last, j, _, diagonal = visits[n]
                m_ref, acc_ref = m_sc.at[slot], acc_sc.at[slot]
                if first:
                    m_ref[...] = jnp.full(m_ref.shape, -jnp.inf, F32)
                    acc_ref[...] = jnp.zeros(acc_ref.shape, F32)
            for c in range(n_chunks):
                if n + 1 < len(visits):
                    nqi, _, _, _, nj, nbias, ndiag = visits[n + 1]
                    produce(c, nqi, nj, nbias, (n + 1) % 2, ndiag)
                if n >= 0:
                    consume(c, j, n % 2, m_ref, acc_ref, diagonal)
            if n >= 0 and last:
                finalize(qi, acc_ref)

    for step in range((n_q_tiles + 1) // 2):
        pair = sorted({step, n_q_tiles - 1 - step})
        pl.when(pl.program_id(2) == step)(functools.partial(run, pair))


def _diff_attention(q, k, vt, bias_tiles, lq, g_col, lambda_init):
    _, b, s, _ = q.shape
    tile = bias_tiles.shape[-1]
    tiles = s // tile
    hd = 2 * A_QK_DIM
    pipelined = (2 * s * hd * 2 + s * hd * 2 + s * A_VT_ROWS * 2 + 2 * tile * tile * 4
                 + s * A_V_DIM * 2)
    scratch = 2 * tile * 2 * tile * 4 + 2 * 8 * (2 * tile) * 4 + 2 * A_VT_ROWS * 2 * tile * 4
    return pl.pallas_call(
        functools.partial(_attn_kernel, tile=tile, n_q_tiles=tiles, lambda_init=lambda_init),
        out_shape=jax.ShapeDtypeStruct((b, s, A_WIDTH), BF16),
        grid=(b, A_HEADS, (tiles + 1) // 2),
        in_specs=[
            pl.BlockSpec((2, 1, s, hd), lambda bi, h, i: (0, bi, 0, h)),
            pl.BlockSpec((1, s, hd), lambda bi, h, i: (bi, 0, h)),
            pl.BlockSpec((1, 1, tiles, A_VT_ROWS, tile), lambda bi, h, i: (bi, h, 0, 0, 0)),
            pl.BlockSpec((1, 2, tile, tile), lambda bi, h, i: (h, 0, 0, 0)),
            _resident(lq.shape),
            _resident(g_col.shape),
        ],
        out_specs=pl.BlockSpec((1, s, A_V_DIM), lambda bi, h, i: (bi, 0, h)),
        scratch_shapes=[
            pltpu.VMEM((2, tile, 2 * tile), F32),
            pltpu.VMEM((2, 1, 2 * tile), F32),
            pltpu.VMEM((2, A_VT_ROWS, 2 * tile), F32),
        ],
        compiler_params=_params(("parallel", "parallel", "arbitrary"),
                                _vmem_limit(pipelined, 128 * 128 * 4, scratch + 6 * tile * (2 * tile) * 4)),
        name="diff_attn",
    )(q, k, vt, bias_tiles, lq, g_col)


def _even_out_kernel(h_ref, o_ref, u_ref, uh_ref, pw_ref, ps_ref, w_ref, out_ref, *, tm):
    i = pl.program_id(1)
    u = u_ref[0].astype(F32)
    halo = jnp.where(i > 0, uh_ref[0].astype(F32), 0.0)
    ext = jnp.concatenate([halo, u], axis=0)
    t = i * tm + lax.broadcasted_iota(jnp.int32, (tm, 1), 0)
    ys = []
    for gi, w in enumerate(POOL_WINDOWS):
        cols = slice(gi * POOL_GROUP_DIM, (gi + 1) * POOL_GROUP_DIM)
        a = ext[:, cols]
        sh = 1
        while sh < w:
            a = a + pltpu.roll(a, sh, 0)
            sh *= 2
        inv_count = 1.0 / jnp.minimum(t + 1, w).astype(F32)
        p = a[POOL_HALO:] * inv_count - u[:, cols]
        ys.append((_dot(p.astype(BF16), pw_ref[gi]) * ps_ref[:, cols]).astype(BF16))
    mix = jnp.concatenate([o_ref[0]] + ys, axis=-1)
    out_ref[0] = h_ref[0] + _dot(mix, w_ref[...])


def _even_out(h, o, u, pool_w, pool_scale, w_out):
    b, s, d = h.shape
    tm = min(ROW_TILE, s)
    halo_blocks = tm // POOL_HALO
    return pl.pallas_call(
        functools.partial(_even_out_kernel, tm=tm),
        out_shape=jax.ShapeDtypeStruct(h.shape, F32),
        grid=(b, s // tm),
        in_specs=[
            pl.BlockSpec((1, tm, d), lambda bi, i: (bi, i, 0)),
            pl.BlockSpec((1, tm, A_WIDTH), lambda bi, i: (bi, i, 0)),
            pl.BlockSpec((1, tm, POOL_WIDTH), lambda bi, i: (bi, i, 0)),
            pl.BlockSpec((1, POOL_HALO, POOL_WIDTH),
                         lambda bi, i: (bi, jnp.maximum(i * halo_blocks - 1, 0), 0)),
            _resident(pool_w.shape),
            _resident(pool_scale.shape),
            _resident(w_out.shape),
        ],
        out_specs=pl.BlockSpec((1, tm, d), lambda bi, i: (bi, i, 0)),
        compiler_params=_params(("parallel", "parallel"),
                                _vmem_limit(2 * tm * d * 4 + 2 * tm * 512 * 2 + POOL_HALO * 512 * 2,
                                            (pool_w.size + w_out.size) * 2 + pool_scale.size * 4,
                                            4 * tm * d * 4)),
        name="even_out",
    )(h, o, u, u, pool_w, pool_scale, w_out)


def _odd_kernel(h_ref, g_ref, win_ref, cw_ref, wout_ref, out_ref, zprev_sc, *, tm):
    i = pl.program_id(1)
    h = h_ref[0]
    d = h.shape[-1]
    hn = _rms(h, g_ref[...]).astype(BF16)
    b_gate = _dot(hn, win_ref[:, 0:d])
    z = _dot(hn, win_ref[:, d:2 * d]) * _dot(hn, win_ref[:, 2 * d:3 * d])

    @pl.when(i == 0)
    def _():
        zprev_sc[...] = jnp.zeros(zprev_sc.shape, F32)

    ext = jnp.concatenate([zprev_sc[...], z], axis=0)
    zprev_sc[...] = z[tm - CONV_HALO:]
    cw = cw_ref[...]
    y = z * cw[CONV_WIDTH - 1:CONV_WIDTH]
    for tap in range(CONV_WIDTH - 1):
        back = CONV_WIDTH - 1 - tap
        y = y + pltpu.roll(ext, back, 0)[CONV_HALO:] * cw[tap:tap + 1]
    out_ref[0] = h + _dot((b_gate * y).astype(BF16), wout_ref[...])


def _odd_mixer(h, g, w_in, conv_w, w_out):
    b, s, d = h.shape
    tm = min(ROW_TILE, s)
    return pl.pallas_call(
        functools.partial(_odd_kernel, tm=tm),
        out_shape=jax.ShapeDtypeStruct(h.shape, F32),
        grid=(b, s // tm),
        in_specs=[
            pl.BlockSpec((1, tm, d), lambda bi, i: (bi, i, 0)),
            _resident(g.shape),
            _resident(w_in.shape),
            _resident(conv_w.shape),
            _resident(w_out.shape),
        ],
        out_specs=pl.BlockSpec((1, tm, d), lambda bi, i: (bi, i, 0)),
        scratch_shapes=[pltpu.VMEM((CONV_HALO, d), F32)],
        compiler_params=_params(("arbitrary", "arbitrary"),
                                _vmem_limit(2 * tm * d * 4, (w_in.size + w_out.size) * 2,
                                            8 * tm * d * 4)),
        name="odd_mixer",
    )(h, g, w_in, conv_w, w_out)


def _xattn_kernel(h_ref, g_ref, wq_ref, kv_ref, wo_ref, out_ref):
    h = h_ref[0]
    d = h.shape[-1]
    hd = d // X_HEADS
    hn = _rms(h, g_ref[...]).astype(BF16)
    q = (_dot(hn, wq_ref[...]) * (hd ** -0.5)).astype(BF16)
    outs = []
    for head in range(X_HEADS):
        cols = slice(head * hd, (head + 1) * hd)
        s = _dot_nt(q[:, cols], kv_ref[0, 0, :, cols])
        p = jnp.exp(s - jnp.max(s, axis=-1, keepdims=True))
        inv_l = 1.0 / jnp.sum(p, axis=-1, keepdims=True)
        v = kv_ref[0, 0, :, d + head * hd:d + (head + 1) * hd]
        outs.append((_dot(p.astype(BF16), v) * inv_l).astype(BF16))
    out_ref[0] = h + _dot(jnp.concatenate(outs, axis=-1), wo_ref[...])


def _xattn(h, g, wq, kv, layer, wo):
    b, s, d = h.shape
    n_mem = kv.shape[2]
    tm = min(ROW_TILE, s)
    return pl.pallas_call(
        _xattn_kernel,
        out_shape=jax.ShapeDtypeStruct(h.shape, F32),
        grid=(b, s // tm),
        in_specs=[
            pl.BlockSpec((1, tm, d), lambda bi, i: (bi, i, 0)),
            _resident(g.shape),
            _resident(wq.shape),
            pl.BlockSpec((1, 1, n_mem, 2 * d), lambda bi, i: (layer, bi, 0, 0)),
            _resident(wo.shape),
        ],
        out_specs=pl.BlockSpec((1, tm, d), lambda bi, i: (bi, i, 0)),
        compiler_params=_params(("parallel", "parallel"),
                                _vmem_limit(2 * tm * d * 4 + n_mem * 2 * d * 2,
                                            (wq.size + wo.size) * 2, 6 * tm * d * 4)),
        name="xattn",
    )(h, g, wq, kv, wo)


def _mlp_kernel(h_ref, g_ref, w1_ref, w2_ref, gf_ref, out_ref, *, chunk, final_norm):
    h = h_ref[...]
    hn = _rms(h, g_ref[...]).astype(BF16)
    acc = h
    for c in range(w1_ref.shape[1] // chunk):
        a = jnp.maximum(_dot(hn, w1_ref[:, c * chunk:(c + 1) * chunk]), 0.0)
        acc = acc + _dot((a * a).astype(BF16), w2_ref[c * chunk:(c + 1) * chunk, :])
    out_ref[...] = _rms(acc, gf_ref[...]) if final_norm else acc


def _mlp(h2d, g, w1, w2, gf, final_norm):
    rows, d = h2d.shape
    tm = min(ROW_TILE, rows)
    chunk = min(1024, w1.shape[1])
    return pl.pallas_call(
        functools.partial(_mlp_kernel, chunk=chunk, final_norm=final_norm),
        out_shape=jax.ShapeDtypeStruct(h2d.shape, F32),
        grid=(rows // tm,),
        in_specs=[
            pl.BlockSpec((tm, d), lambda i: (i, 0)),
            _resident(g.shape),
            _resident(w1.shape),
            _resident(w2.shape),
            _resident(gf.shape),
        ],
        out_specs=pl.BlockSpec((tm, d), lambda i: (i, 0)),
        compiler_params=_params(("parallel",),
                                _vmem_limit(2 * tm * d * 4, (w1.size + w2.size) * 2,
                                            2 * tm * d * 4 + 3 * tm * chunk * 4)),
        name="mlp",
    )(h2d, g, w1, w2, gf)


def _t5_bucket(n):
    small = n < REL_MAX_EXACT
    nf = jnp.maximum(n, 1).astype(F32)
    large = REL_MAX_EXACT + (jnp.log(nf / REL_MAX_EXACT) / math.log(REL_MAX_DIST / REL_MAX_EXACT)
                             * (REL_BUCKETS - REL_MAX_EXACT)).astype(jnp.int32)
    return jnp.where(small, n, jnp.minimum(large, REL_BUCKETS - 1))


def _bias_tiles(rel_bias, tile):
    assert tile + 1 >= REL_MAX_DIST
    rb = rel_bias.astype(F32)
    rb = (rb - rb[REL_BUCKETS - 1]).T[:, :, None, None]
    pos = jnp.arange(tile, dtype=jnp.int32)
    out = []
    for delta in (0, 1):
        dist = delta * tile + pos[None, :] - pos[:, None]
        bucket = _t5_bucket(jnp.maximum(dist, 0))
        vals = jnp.zeros((rb.shape[0], tile, tile), F32)
        for bkt in range(REL_BUCKETS - 1):
            vals = jnp.where(bucket == bkt, rb[:, bkt], vals)
        out.append(jnp.where(dist >= 0, vals * LOG2E, NEG_BIG))
    return jnp.stack(out, axis=1)


def kernel(x, mem, rel_bias, mem_norm_g, norm_mix_g, norm_xattn_g, norm_mlp_g, final_norm_g,
           ab_w_in, ab_w_out, lambda_q1, lambda_k1, lambda_q2, lambda_k2, subln_g, pool_w,
           pool_scale, conv_w_in, conv_w, conv_w_out, xattn_wq, xattn_wkv, xattn_wo, mlp_w1, mlp_w2):
    b, s, d = x.shape
    depth = norm_mix_g.shape[0]
    rows = b * s
    tile = min(ATTN_TILE, s)

    ab_w_in, ab_w_out, pool_w, conv_w_in, conv_w_out, xattn_wq, xattn_wkv, xattn_wo, mlp_w1, mlp_w2 = (
        w.astype(BF16) for w in (ab_w_in, ab_w_out, pool_w, conv_w_in, conv_w_out, xattn_wq,
                                 xattn_wkv, xattn_wo, mlp_w1, mlp_w2))

    bias_tiles = _bias_tiles(rel_bias, tile)
    kv = _memkv(mem.reshape(b * mem.shape[1], d), mem_norm_g.reshape(1, d), xattn_wkv)
    kv = kv.reshape(depth, b, mem.shape[1], 2 * d)

    h = x
    for l in range(depth):
        i = l // 2
        g_mix = norm_mix_g[l].reshape(1, d)
        if l % 2 == 0:
            lambda_init = 0.8 - 0.6 * math.exp(-0.3 * l)
            q, k, vt, u = _even_in(h.reshape(rows, d), g_mix, ab_w_in[i], b, tile)
            lq = jnp.stack([lambda_q1[i], lambda_k1[i], lambda_q2[i], lambda_k2[i]]).astype(F32)
            o = _diff_attention(q.reshape(2, b, s, -1), k.reshape(b, s, -1), vt,
                                bias_tiles, lq, subln_g[i].reshape(-1, 1), lambda_init)
            h = _even_out(h, o, u.reshape(b, s, -1), pool_w[i], pool_scale[i].reshape(1, -1), ab_w_out[i])
        else:
            h = _odd_mixer(h, g_mix, conv_w_in[i], conv_w[i], conv_w_out[i])
        h = _xattn(h, norm_xattn_g[l].reshape(1, d), xattn_wq[l], kv, l, xattn_wo[l])
        h = _mlp(h.reshape(rows, d), norm_mlp_g[l].reshape(1, d), mlp_w1[l], mlp_w2[l],
                 final_norm_g.reshape(1, d), l == depth - 1).reshape(b, s, d)
    return h
```

```python
import functools
import math

import jax
import jax.numpy as jnp
from jax import lax
from jax.experimental import pallas as pl
from jax.experimental.pallas import tpu as pltpu

F32 = jnp.float32
BF16 = jnp.bfloat16

EPS = 1e-6
A_HEADS = 4
A_QK_DIM = 64
A_V_DIM = 2 * A_QK_DIM
A_WIDTH = A_HEADS * A_V_DIM
A_VT_ROWS = A_V_DIM + 16
POOL_WINDOWS = (2, 4, 8, 16)
POOL_GROUP_DIM = 128
POOL_WIDTH = len(POOL_WINDOWS) * POOL_GROUP_DIM
POOL_HALO = 16
CONV_WIDTH = 3
CONV_HALO = 8
X_HEADS = 4
REL_BUCKETS = 32
REL_MAX_EXACT = REL_BUCKETS // 2
REL_MAX_DIST = 128

V7X_VMEM_LIMIT_CAP = 56 * 1024 * 1024

ROW_TILE = 1024
ATTN_TILE = 512
ATTN_QUERY_CHUNK = 256
LOG2E = math.log2(math.e)
NEG_BIG = -1e30


def _vmem_limit(pipelined_bytes, resident_bytes, live_bytes):
    return min(2 * pipelined_bytes + resident_bytes + live_bytes, V7X_VMEM_LIMIT_CAP)


def _params(semantics, vmem):
    return pltpu.CompilerParams(dimension_semantics=semantics, vmem_limit_bytes=vmem)


def _resident(shape):
    return pl.BlockSpec(shape, lambda *_: (0,) * len(shape), pipeline_mode=pl.Buffered(1))


def _rms(x, g):
    return x * lax.rsqrt(jnp.mean(x * x, axis=-1, keepdims=True) + EPS) * g


def _dot(a, b):
    return jnp.dot(a, b, preferred_element_type=F32)


def _dot_nt(a, b):
    return lax.dot_general(a, b, (((1,), (1,)), ((), ())), preferred_element_type=F32)


def _mem_fold_kernel(x_ref, g_ref, wkv_ref, wq_ref, wo_ref, sw_ref, vw_ref):
    d = x_ref.shape[-1]
    hd = d // X_HEADS
    xn = _rms(x_ref[0], g_ref[...]).astype(BF16)
    kv = _dot(xn, wkv_ref[0]).astype(BF16)
    n_mem = kv.shape[0]
    for head in range(X_HEADS):
        cols = slice(head * hd, (head + 1) * hd)
        keys = slice(head * n_mem, (head + 1) * n_mem)
        sw = _dot_nt(wq_ref[0, :, cols], kv[:, cols]) * (hd ** -0.5 * LOG2E)
        sw_ref[0, 0, :, keys] = sw.astype(BF16)
        vw_ref[0, 0, keys, :] = _dot(kv[:, d + head * hd:d + (head + 1) * hd], wo_ref[0, cols, :]).astype(BF16)


def _mem_fold(mem, g, wkv, wq, wo):
    b, n_mem, d = mem.shape
    depth = wkv.shape[0]
    n = X_HEADS * n_mem
    layer_spec = lambda shape: pl.BlockSpec((1,) + shape, lambda l, bi: (l, 0, 0))
    return pl.pallas_call(
        _mem_fold_kernel,
        out_shape=(jax.ShapeDtypeStruct((depth, b, d, n), BF16),
                   jax.ShapeDtypeStruct((depth, b, n, d), BF16)),
        grid=(depth, b),
        in_specs=[
            pl.BlockSpec((1, n_mem, d), lambda l, bi: (bi, 0, 0)),
            _resident((1, d)),
            layer_spec((d, 2 * d)),
            layer_spec((d, d)),
            layer_spec((d, d)),
        ],
        out_specs=(pl.BlockSpec((1, 1, d, n), lambda l, bi: (l, bi, 0, 0)),
                   pl.BlockSpec((1, 1, n, d), lambda l, bi: (l, bi, 0, 0))),
        compiler_params=_params(("parallel", "parallel"),
                                _vmem_limit(n_mem * d * 4 + 4 * d * d * 2 + 2 * d * n * 2, d * 4,
                                            n_mem * 2 * d * 6 + 2 * d * n * 4)),
        name="mem_fold",
    )(mem, g, wkv, wq, wo)


def _even_in_kernel(x_ref, g_ref, w_ref, q_ref, k_ref, vt_ref, u_ref):
    xn = _rms(x_ref[...], g_ref[...]).astype(BF16)
    w = A_HEADS * 2 * A_QK_DIM
    q = _dot(xn, w_ref[:, 0:w]) * (A_QK_DIM ** -0.5 * LOG2E)
    first = (lax.broadcasted_iota(jnp.int32, q.shape, 1) % (2 * A_QK_DIM)) < A_QK_DIM
    q_ref[0] = jnp.where(first, q, 0.0).astype(BF16)
    q_ref[1] = jnp.where(first, 0.0, q).astype(BF16)
    k_ref[...] = _dot(xn, w_ref[:, w:2 * w]).astype(BF16)
    vt = _dot(xn, w_ref[:, 2 * w:2 * w + A_WIDTH]).T.astype(BF16)
    vt_ref[0, :, 0, 0:A_V_DIM] = vt.reshape(A_HEADS, A_V_DIM, vt.shape[-1])
    pad_shape = (A_HEADS, A_VT_ROWS - A_V_DIM, vt.shape[-1])
    ones_row = lax.broadcasted_iota(jnp.int32, pad_shape, 1) == 0
    vt_ref[0, :, 0, A_V_DIM:A_VT_ROWS] = jnp.where(ones_row, 1.0, 0.0).astype(BF16)
    u_ref[...] = _dot(xn, w_ref[:, 2 * w + A_WIDTH:]).astype(BF16)


def _even_in(h2d, g, w_in, batch, tile):
    rows, d = h2d.shape
    n = w_in.shape[1]
    c = n // 4
    tiles = rows // batch // tile
    row_spec = pl.BlockSpec((tile, c), lambda i: (i, 0))
    return pl.pallas_call(
        _even_in_kernel,
        out_shape=(jax.ShapeDtypeStruct((2, rows, c), BF16),
                   jax.ShapeDtypeStruct((rows, c), BF16),
                   jax.ShapeDtypeStruct((batch, A_HEADS, tiles, A_VT_ROWS, tile), BF16),
                   jax.ShapeDtypeStruct((rows, c), BF16)),
        grid=(rows // tile,),
        in_specs=[pl.BlockSpec((tile, d), lambda i: (i, 0)), _resident((1, d)), _resident((d, n))],
        out_specs=(pl.BlockSpec((2, tile, c), lambda i: (0, i, 0)),
                   row_spec,
                   pl.BlockSpec((1, A_HEADS, 1, A_VT_ROWS, tile), lambda i: (i // tiles, 0, i % tiles, 0, 0)),
                   row_spec),
        compiler_params=_params(("parallel",),
                                _vmem_limit(tile * d * 4 + 5 * tile * c * 2, d * n * 2 + d * 4, 4 * tile * d * 4)),
        name="even_in",
    )(h2d, g, w_in)


def _attn_kernel(q_ref, k_ref, vt_ref, bias_ref, lq_ref, g_ref, o_ref, s_sc, m_sc, acc_sc,
                 *, tile, n_q_tiles, lambda_init):
    n_chunks = 2 * tile // ATTN_QUERY_CHUNK
    chunk_cols = [slice(c * ATTN_QUERY_CHUNK, (c + 1) * ATTN_QUERY_CHUNK) for c in range(n_chunks)]

    def live_keys(c, diagonal):
        q_end = (c * ATTN_QUERY_CHUNK) % tile + ATTN_QUERY_CHUNK
        return q_end if diagonal else tile

    def produce(c, qi, j, bias, buf, diagonal):
        keys = live_keys(c, diagonal)
        q0 = qi * tile + (c * ATTN_QUERY_CHUNK) % tile
        qc = q_ref[(c * ATTN_QUERY_CHUNK) // tile, 0, q0:q0 + ATTN_QUERY_CHUNK, :]
        s = _dot_nt(k_ref[0, j * tile:j * tile + keys, :], qc)
        if bias is not None:
            b0 = (c * ATTN_QUERY_CHUNK) % tile
            s = s + bias[0:keys, b0:b0 + ATTN_QUERY_CHUNK]
        s_sc[buf, 0:keys, chunk_cols[c]] = s

    def consume(c, j, buf, m_ref, acc_ref, diagonal):
        keys = live_keys(c, diagonal)
        cols = chunk_cols[c]
        s = s_sc[buf, 0:keys, cols]
        m_prev = m_ref[:, cols]
        m_new = jnp.maximum(m_prev, jnp.max(s, axis=0, keepdims=True))
        alpha = jnp.exp2(m_prev - m_new)
        p = jnp.exp2(s - m_new).astype(BF16)
        acc_ref[:, cols] = alpha * acc_ref[:, cols] + _dot(vt_ref[0, 0, j, :, 0:keys], p)
        m_ref[:, cols] = m_new

    def finalize(qi, acc_ref):
        o = acc_ref[0:A_V_DIM] * (1.0 / acc_ref[A_V_DIM:A_V_DIM + 1])
        lq = lq_ref[...]
        lam = (jnp.exp(jnp.sum(lq[0:1] * lq[1:2], axis=-1, keepdims=True))
               - jnp.exp(jnp.sum(lq[2:3] * lq[3:4], axis=-1, keepdims=True)) + lambda_init)
        od = o[:, :tile] - lam * o[:, tile:]
        y = od * lax.rsqrt(jnp.mean(od * od, axis=0, keepdims=True) + EPS) * g_ref[...]
        o_ref[0, qi * tile:(qi + 1) * tile, :] = (y * (1.0 - lambda_init)).T.astype(BF16)

    def run(q_tiles):
        visits = []
        for slot, qi in enumerate(q_tiles):
            tiles = [(qi, bias_ref.at[0, 0], True)]
            if qi >= 1:
                tiles.append((qi - 1, bias_ref.at[0, 1], False))
            tiles += [(j, None, False) for j in range(qi - 1)]
            visits += [(qi, slot % 2, n == 0, n == len(tiles) - 1) + t for n, t in enumerate(tiles)]

        for n in range(-1, len(visits)):
            if n >= 0:
                qi, slot, first, last, j, _, diagonal = visits[n]
                m_ref, acc_ref = m_sc.at[slot], acc_sc.at[slot]
                if first:
                    m_ref[...] = jnp.full(m_ref.shape, -jnp.inf, F32)
                    acc_ref[...] = jnp.zeros(acc_ref.shape, F32)
            for c in range(n_chunks):
                if n + 1 < len(visits):
                    nqi, _, _, _, nj, nbias, ndiag = visits[n + 1]
                    produce(c, nqi, nj, nbias, (n + 1) % 2, ndiag)
                if n >= 0:
                    consume(c, j, n % 2, m_ref, acc_ref, diagonal)
            if n >= 0 and last:
                finalize(qi, acc_ref)

    for step in range((n_q_tiles + 1) // 2):
        pair = sorted({step, n_q_tiles - 1 - step})
        pl.when(pl.program_id(2) == step)(functools.partial(run, pair))


def _diff_attention(q, k, vt, bias_tiles, lq, g_col, lambda_init):
    _, b, s, _ = q.shape
    tile = bias_tiles.shape[-1]
    tiles = s // tile
    hd = 2 * A_QK_DIM
    pipelined = (2 * s * hd * 2 + s * hd * 2 + s * A_VT_ROWS * 2 + 2 * tile * tile * 4
                 + s * A_V_DIM * 2)
    scratch = 2 * tile * 2 * tile * 4 + 2 * 8 * (2 * tile) * 4 + 2 * A_VT_ROWS * 2 * tile * 4
    return pl.pallas_call(
        functools.partial(_attn_kernel, tile=tile, n_q_tiles=tiles, lambda_init=lambda_init),
        out_shape=jax.ShapeDtypeStruct((b, s, A_WIDTH), BF16),
        grid=(b, A_HEADS, (tiles + 1) // 2),
        in_specs=[
            pl.BlockSpec((2, 1, s, hd), lambda bi, h, i: (0, bi, 0, h)),
            pl.BlockSpec((1, s, hd), lambda bi, h, i: (bi, 0, h)),
            pl.BlockSpec((1, 1, tiles, A_VT_ROWS, tile), lambda bi, h, i: (bi, h, 0, 0, 0)),
            pl.BlockSpec((1, 2, tile, tile), lambda bi, h, i: (h, 0, 0, 0)),
            _resident(lq.shape),
            _resident(g_col.shape),
        ],
        out_specs=pl.BlockSpec((1, s, A_V_DIM), lambda bi, h, i: (bi, 0, h)),
        scratch_shapes=[
            pltpu.VMEM((2, tile, 2 * tile), F32),
            pltpu.VMEM((2, 1, 2 * tile), F32),
            pltpu.VMEM((2, A_VT_ROWS, 2 * tile), F32),
        ],
        compiler_params=_params(("parallel", "parallel", "arbitrary"),
                                _vmem_limit(pipelined, 128 * 128 * 4, scratch + 6 * tile * (2 * tile) * 4)),
        name="diff_attn",
    )(q, k, vt, bias_tiles, lq, g_col)


def _pool_fold_kernel(pw_ref, ps_ref, w_ref, o_ref):
    o_ref[0, 0:A_WIDTH, :] = w_ref[0, 0:A_WIDTH, :]
    for gi in range(len(POOL_WINDOWS)):
        cols = slice(gi * POOL_GROUP_DIM, (gi + 1) * POOL_GROUP_DIM)
        rows = slice(A_WIDTH + gi * POOL_GROUP_DIM, A_WIDTH + (gi + 1) * POOL_GROUP_DIM)
        pw = (pw_ref[0, gi] * ps_ref[0, :, cols]).astype(BF16)
        o_ref[0, rows, :] = _dot(pw, w_ref[0, rows, :]).astype(BF16)


def _pool_fold(pool_w, pool_scale, w_out):
    n, rows, d = w_out.shape
    return pl.pallas_call(
        _pool_fold_kernel,
        out_shape=jax.ShapeDtypeStruct(w_out.shape, BF16),
        grid=(n,),
        in_specs=[
            pl.BlockSpec((1,) + pool_w.shape[1:], lambda i: (i, 0, 0, 0)),
            pl.BlockSpec((1, 1, POOL_WIDTH), lambda i: (i, 0, 0)),
            pl.BlockSpec((1, rows, d), lambda i: (i, 0, 0)),
        ],
        out_specs=pl.BlockSpec((1, rows, d), lambda i: (i, 0, 0)),
        compiler_params=_params(("parallel",),
                                _vmem_limit(pool_w[0].size * 4 + POOL_WIDTH * 4 + 2 * rows * d * 2, 0,
                                            POOL_WIDTH * d * 4)),
        name="pool_fold",
    )(pool_w, pool_scale.reshape(n, 1, POOL_WIDTH), w_out)


def _even_out_kernel(h_ref, o_ref, u_ref, uh_ref, w_ref, out_ref, *, tm):
    i = pl.program_id(1)
    u = u_ref[0].astype(F32)
    halo = jnp.where(i > 0, uh_ref[0].astype(F32), 0.0)
    ext = jnp.concatenate([halo, u], axis=0)
    t = i * tm + lax.broadcasted_iota(jnp.int32, (tm, 1), 0)
    pooled = []
    for gi, w in enumerate(POOL_WINDOWS):
        cols = slice(gi * POOL_GROUP_DIM, (gi + 1) * POOL_GROUP_DIM)
        a = ext[:, cols]
        sh = 1
        while sh < w:
            a = a + pltpu.roll(a, sh, 0)
            sh *= 2
        inv_count = 1.0 / jnp.minimum(t + 1, w).astype(F32)
        pooled.append((a[POOL_HALO:] * inv_count - u[:, cols]).astype(BF16))
    mix = jnp.concatenate([o_ref[0]] + pooled, axis=-1)
    out_ref[0] = h_ref[0] + _dot(mix, w_ref[...])


def _even_out(h, o, u, w_mix):
    b, s, d = h.shape
    tm = min(ROW_TILE, s)
    halo_blocks = tm // POOL_HALO
    return pl.pallas_call(
        functools.partial(_even_out_kernel, tm=tm),
        out_shape=jax.ShapeDtypeStruct(h.shape, F32),
        grid=(b, s // tm),
        in_specs=[
            pl.BlockSpec((1, tm, d), lambda bi, i: (bi, i, 0)),
            pl.BlockSpec((1, tm, A_WIDTH), lambda bi, i: (bi, i, 0)),
            pl.BlockSpec((1, tm, POOL_WIDTH), lambda bi, i: (bi, i, 0)),
            pl.BlockSpec((1, POOL_HALO, POOL_WIDTH),
                         lambda bi, i: (bi, jnp.maximum(i * halo_blocks - 1, 0), 0)),
            _resident(w_mix.shape),
        ],
        out_specs=pl.BlockSpec((1, tm, d), lambda bi, i: (bi, i, 0)),
        compiler_params=_params(("parallel", "parallel"),
                                _vmem_limit(2 * tm * d * 4 + 2 * tm * 512 * 2 + POOL_HALO * 512 * 2,
                                            w_mix.size * 2, 4 * tm * d * 4)),
        name="even_out",
    )(h, o, u, u, w_mix)


def _odd_kernel(h_ref, g_ref, win_ref, cw_ref, wout_ref, out_ref, zprev_sc, *, tm):
    i = pl.program_id(1)
    h = h_ref[0]
    d = h.shape[-1]
    hn = _rms(h, g_ref[...]).astype(BF16)
    b_gate = _dot(hn, win_ref[:, 0:d])
    z = _dot(hn, win_ref[:, d:2 * d]) * _dot(hn, win_ref[:, 2 * d:3 * d])

    @pl.when(i == 0)
    def _():
        zprev_sc[...] = jnp.zeros(zprev_sc.shape, F32)

    ext = jnp.concatenate([zprev_sc[...], z], axis=0)
    zprev_sc[...] = z[tm - CONV_HALO:]
    cw = cw_ref[...]
    y = z * cw[CONV_WIDTH - 1:CONV_WIDTH]
    for tap in range(CONV_WIDTH - 1):
        back = CONV_WIDTH - 1 - tap
        y = y + pltpu.roll(ext, back, 0)[CONV_HALO:] * cw[tap:tap + 1]
    out_ref[0] = h + _dot((b_gate * y).astype(BF16), wout_ref[...])


def _odd_mixer(h, g, w_in, conv_w, w_out):
    b, s, d = h.shape
    tm = min(ROW_TILE, s)
    return pl.pallas_call(
        functools.partial(_odd_kernel, tm=tm),
        out_shape=jax.ShapeDtypeStruct(h.shape, F32),
        grid=(b, s // tm),
        in_specs=[
            pl.BlockSpec((1, tm, d), lambda bi, i: (bi, i, 0)),
            _resident(g.shape),
            _resident(w_in.shape),
            _resident(conv_w.shape),
            _resident(w_out.shape),
        ],
        out_specs=pl.BlockSpec((1, tm, d), lambda bi, i: (bi, i, 0)),
        scratch_shapes=[pltpu.VMEM((CONV_HALO, d), F32)],
        compiler_params=_params(("arbitrary", "arbitrary"),
                                _vmem_limit(2 * tm * d * 4, (w_in.size + w_out.size) * 2,
                                            8 * tm * d * 4)),
        name="odd_mixer",
    )(h, g, w_in, conv_w, w_out)


def _xattn_kernel(h_ref, g_ref, sw_ref, vw_ref, out_ref):
    h = h_ref[0]
    hn = _rms(h, g_ref[...]).astype(BF16)
    s = _dot(hn, sw_ref[0, 0])
    n_mem = s.shape[-1] // X_HEADS
    probs = []
    for head in range(X_HEADS):
        sh = s[:, head * n_mem:(head + 1) * n_mem]
        p = jnp.exp2(sh - jnp.max(sh, axis=-1, keepdims=True))
        probs.append((p * (1.0 / jnp.sum(p, axis=-1, keepdims=True))).astype(BF16))
    out_ref[0] = h + _dot(jnp.concatenate(probs, axis=-1), vw_ref[0, 0])


def _xattn(h, g, score_w, value_w, layer):
    b, s, d = h.shape
    n = score_w.shape[-1]
    tm = min(ROW_TILE, s)
    return pl.pallas_call(
        _xattn_kernel,
        out_shape=jax.ShapeDtypeStruct(h.shape, F32),
        grid=(b, s // tm),
        in_specs=[
            pl.BlockSpec((1, tm, d), lambda bi, i: (bi, i, 0)),
            _resident(g.shape),
            pl.BlockSpec((1, 1, d, n), lambda bi, i: (layer, bi, 0, 0)),
            pl.BlockSpec((1, 1, n, d), lambda bi, i: (layer, bi, 0, 0)),
        ],
        out_specs=pl.BlockSpec((1, tm, d), lambda bi, i: (bi, i, 0)),
        compiler_params=_params(("parallel", "parallel"),
                                _vmem_limit(2 * tm * d * 4 + 2 * d * n * 2, d * 4,
                                            2 * tm * d * 4 + 3 * tm * n * 4)),
        name="xattn",
    )(h, g, score_w, value_w)


def _mlp_kernel(h_ref, g_ref, w1_ref, w2_ref, gf_ref, out_ref, *, chunk, final_norm):
    h = h_ref[...]
    hn = _rms(h, g_ref[...]).astype(BF16)
    acc = h
    for c in range(w1_ref.shape[1] // chunk):
        a = jnp.maximum(_dot(hn, w1_ref[:, c * chunk:(c + 1) * chunk]), 0.0)
        acc = acc + _dot((a * a).astype(BF16), w2_ref[c * chunk:(c + 1) * chunk, :])
    out_ref[...] = _rms(acc, gf_ref[...]) if final_norm else acc


def _mlp(h2d, g, w1, w2, gf, final_norm):
    rows, d = h2d.shape
    tm = min(ROW_TILE, rows)
    chunk = min(1024, w1.shape[1])
    return pl.pallas_call(
        functools.partial(_mlp_kernel, chunk=chunk, final_norm=final_norm),
        out_shape=jax.ShapeDtypeStruct(h2d.shape, F32),
        grid=(rows // tm,),
        in_specs=[
            pl.BlockSpec((tm, d), lambda i: (i, 0)),
            _resident(g.shape),
            _resident(w1.shape),
            _resident(w2.shape),
            _resident(gf.shape),
        ],
        out_specs=pl.BlockSpec((tm, d), lambda i: (i, 0)),
        compiler_params=_params(("parallel",),
                                _vmem_limit(2 * tm * d * 4, (w1.size + w2.size) * 2,
                                            2 * tm * d * 4 + 3 * tm * chunk * 4)),
        name="mlp",
    )(h2d, g, w1, w2, gf)


def _t5_bucket(n):
    small = n < REL_MAX_EXACT
    nf = jnp.maximum(n, 1).astype(F32)
    large = REL_MAX_EXACT + (jnp.log(nf / REL_MAX_EXACT) / math.log(REL_MAX_DIST / REL_MAX_EXACT)
                             * (REL_BUCKETS - REL_MAX_EXACT)).astype(jnp.int32)
    return jnp.where(small, n, jnp.minimum(large, REL_BUCKETS - 1))


def _bias_tiles(rel_bias, tile):
    assert tile + 1 >= REL_MAX_DIST
    rb = rel_bias.astype(F32)
    rb = (rb - rb[REL_BUCKETS - 1]).T[:, :, None, None]
    pos = jnp.arange(tile, dtype=jnp.int32)
    out = []
    for delta in (0, 1):
        dist = delta * tile + pos[None, :] - pos[:, None]
        bucket = _t5_bucket(jnp.maximum(dist, 0))
        vals = jnp.zeros((rb.shape[0], tile, tile), F32)
        for bkt in range(REL_BUCKETS - 1):
            vals = jnp.where(bucket == bkt, rb[:, bkt], vals)
        out.append(jnp.where(dist >= 0, vals * LOG2E, NEG_BIG))
    return jnp.stack(out, axis=1)


def kernel(x, mem, rel_bias, mem_norm_g, norm_mix_g, norm_xattn_g, norm_mlp_g, final_norm_g,
           ab_w_in, ab_w_out, lambda_q1, lambda_k1, lambda_q2, lambda_k2, subln_g, pool_w,
           pool_scale, conv_w_in, conv_w, conv_w_out, xattn_wq, xattn_wkv, xattn_wo, mlp_w1, mlp_w2):
    b, s, d = x.shape
    depth = norm_mix_g.shape[0]
    rows = b * s
    tile = min(ATTN_TILE, s)

    ab_w_in, ab_w_out, conv_w_in, conv_w_out, xattn_wq, xattn_wkv, xattn_wo, mlp_w1, mlp_w2 = (
        w.astype(BF16) for w in (ab_w_in, ab_w_out, conv_w_in, conv_w_out, xattn_wq,
                                 xattn_wkv, xattn_wo, mlp_w1, mlp_w2))

    bias_tiles = _bias_tiles(rel_bias, tile)
    score_w, value_w = _mem_fold(mem, mem_norm_g.reshape(1, d), xattn_wkv, xattn_wq, xattn_wo)
    ab_w_mix = _pool_fold(pool_w.astype(F32), pool_scale.astype(F32), ab_w_out)

    h = x
    for l in range(depth):
        i = l // 2
        g_mix = norm_mix_g[l].reshape(1, d)
        if l % 2 == 0:
            lambda_init = 0.8 - 0.6 * math.exp(-0.3 * l)
            q, k, vt, u = _even_in(h.reshape(rows, d), g_mix, ab_w_in[i], b, tile)
            lq = jnp.stack([lambda_q1[i], lambda_k1[i], lambda_q2[i], lambda_k2[i]]).astype(F32)
            o = _diff_attention(q.reshape(2, b, s, -1), k.reshape(b, s, -1), vt,
                                bias_tiles, lq, subln_g[i].reshape(-1, 1), lambda_init)
            h = _even_out(h, o, u.reshape(b, s, -1), ab_w_mix[i])
        else:
            h = _odd_mixer(h, g_mix, conv_w_in[i], conv_w[i], conv_w_out[i])
        h = _xattn(h, norm_xattn_g[l].reshape(1, d), score_w, value_w, l)
        h = _mlp(h.reshape(rows, d), norm_mlp_g[l].reshape(1, d), mlp_w1[l], mlp_w2[l],
                 final_norm_g.reshape(1, d), l == depth - 1).reshape(b, s, d)
    return h
```

```python
import functools
import math

import jax
import jax.numpy as jnp
from jax import lax
from jax.experimental import pallas as pl
from jax.experimental.pallas import tpu as pltpu

F32 = jnp.float32
BF16 = jnp.bfloat16

EPS = 1e-6
A_HEADS = 4
A_QK_DIM = 64
A_V_DIM = 2 * A_QK_DIM
A_WIDTH = A_HEADS * A_V_DIM
A_VT_ROWS = A_V_DIM + 16
POOL_WINDOWS = (2, 4, 8, 16)
POOL_GROUP_DIM = 128
POOL_WIDTH = len(POOL_WINDOWS) * POOL_GROUP_DIM
POOL_HALO = 16
CONV_WIDTH = 3
CONV_HALO = 8
X_HEADS = 4
REL_BUCKETS = 32
REL_MAX_EXACT = REL_BUCKETS // 2
REL_MAX_DIST = 128

V7X_VMEM_LIMIT_CAP = 56 * 1024 * 1024

ROW_TILE = 1024
ATTN_TILE = 512
ATTN_QUERY_CHUNK = 256
LOG2E = math.log2(math.e)
NEG_BIG = -1e30


def _vmem_limit(pipelined_bytes, resident_bytes, live_bytes):
    return min(2 * pipelined_bytes + resident_bytes + live_bytes, V7X_VMEM_LIMIT_CAP)


def _params(semantics, vmem):
    return pltpu.CompilerParams(dimension_semantics=semantics, vmem_limit_bytes=vmem)


def _resident(shape):
    return pl.BlockSpec(shape, lambda *_: (0,) * len(shape), pipeline_mode=pl.Buffered(1))


def _rms(x, g):
    return x * lax.rsqrt(jnp.mean(x * x, axis=-1, keepdims=True) + EPS) * g


def _dot(a, b):
    return jnp.dot(a, b, preferred_element_type=F32)


def _dot_nt(a, b):
    return lax.dot_general(a, b, (((1,), (1,)), ((), ())), preferred_element_type=F32)


def _mem_fold_kernel(x_ref, g_ref, wkv_ref, wq_ref, wo_ref, sw_ref, vw_ref):
    d = x_ref.shape[-1]
    hd = d // X_HEADS
    xn = _rms(x_ref[0], g_ref[...]).astype(BF16)
    kv = _dot(xn, wkv_ref[0]).astype(BF16)
    n_mem = kv.shape[0]
    for head in range(X_HEADS):
        cols = slice(head * hd, (head + 1) * hd)
        keys = slice(head * n_mem, (head + 1) * n_mem)
        sw = _dot_nt(wq_ref[0, :, cols], kv[:, cols]) * (hd ** -0.5 * LOG2E)
        sw_ref[0, 0, :, keys] = sw.astype(BF16)
        vw_ref[0, 0, keys, :] = _dot(kv[:, d + head * hd:d + (head + 1) * hd], wo_ref[0, cols, :]).astype(BF16)


def _mem_fold(mem, g, wkv, wq, wo):
    b, n_mem, d = mem.shape
    depth = wkv.shape[0]
    n = X_HEADS * n_mem
    layer_spec = lambda shape: pl.BlockSpec((1,) + shape, lambda l, bi: (l, 0, 0))
    return pl.pallas_call(
        _mem_fold_kernel,
        out_shape=(jax.ShapeDtypeStruct((depth, b, d, n), BF16),
                   jax.ShapeDtypeStruct((depth, b, n, d), BF16)),
        grid=(depth, b),
        in_specs=[
            pl.BlockSpec((1, n_mem, d), lambda l, bi: (bi, 0, 0)),
            _resident((1, d)),
            layer_spec((d, 2 * d)),
            layer_spec((d, d)),
            layer_spec((d, d)),
        ],
        out_specs=(pl.BlockSpec((1, 1, d, n), lambda l, bi: (l, bi, 0, 0)),
                   pl.BlockSpec((1, 1, n, d), lambda l, bi: (l, bi, 0, 0))),
        compiler_params=_params(("parallel", "parallel"),
                                _vmem_limit(n_mem * d * 4 + 4 * d * d * 2 + 2 * d * n * 2, d * 4,
                                            n_mem * 2 * d * 6 + 2 * d * n * 4)),
        name="mem_fold",
    )(mem, g, wkv, wq, wo)


def _even_in_kernel(x_ref, g_ref, w_ref, q_ref, k_ref, vt_ref, u_ref):
    xn = _rms(x_ref[...], g_ref[...]).astype(BF16)
    w = A_HEADS * 2 * A_QK_DIM
    q = _dot(xn, w_ref[:, 0:w]) * (A_QK_DIM ** -0.5 * LOG2E)
    first = (lax.broadcasted_iota(jnp.int32, q.shape, 1) % (2 * A_QK_DIM)) < A_QK_DIM
    q_ref[0] = jnp.where(first, q, 0.0).astype(BF16)
    q_ref[1] = jnp.where(first, 0.0, q).astype(BF16)
    k_ref[...] = _dot(xn, w_ref[:, w:2 * w]).astype(BF16)
    vt = _dot(xn, w_ref[:, 2 * w:2 * w + A_WIDTH]).T.astype(BF16)
    tile = vt_ref.shape[-1]
    pad_shape = (A_HEADS, A_VT_ROWS - A_V_DIM, tile)
    ones_row = lax.broadcasted_iota(jnp.int32, pad_shape, 1) == 0
    for t in range(vt_ref.shape[2]):
        vt_ref[0, :, t, 0:A_V_DIM] = vt[:, t * tile:(t + 1) * tile].reshape(A_HEADS, A_V_DIM, tile)
        vt_ref[0, :, t, A_V_DIM:A_VT_ROWS] = jnp.where(ones_row, 1.0, 0.0).astype(BF16)
    u_ref[...] = _dot(xn, w_ref[:, 2 * w + A_WIDTH:]).astype(BF16)


def _even_in(h2d, g, w_in, batch, tile):
    rows, d = h2d.shape
    n = w_in.shape[1]
    c = n // 4
    seq = rows // batch
    tiles = seq // tile
    tm = min(ROW_TILE, seq)
    steps = seq // tm
    row_spec = pl.BlockSpec((tm, c), lambda i: (i, 0))
    return pl.pallas_call(
        _even_in_kernel,
        out_shape=(jax.ShapeDtypeStruct((2, rows, c), BF16),
                   jax.ShapeDtypeStruct((rows, c), BF16),
                   jax.ShapeDtypeStruct((batch, A_HEADS, tiles, A_VT_ROWS, tile), BF16),
                   jax.ShapeDtypeStruct((rows, c), BF16)),
        grid=(rows // tm,),
        in_specs=[pl.BlockSpec((tm, d), lambda i: (i, 0)), _resident((1, d)), _resident((d, n))],
        out_specs=(pl.BlockSpec((2, tm, c), lambda i: (0, i, 0)),
                   row_spec,
                   pl.BlockSpec((1, A_HEADS, tm // tile, A_VT_ROWS, tile),
                                lambda i: (i // steps, 0, i % steps, 0, 0)),
                   row_spec),
        compiler_params=_params(("parallel",),
                                _vmem_limit(tm * d * 4 + 5 * tm * c * 2, d * n * 2 + d * 4, 4 * tm * d * 4)),
        name="even_in",
    )(h2d, g, w_in)


def _attn_step_tiles(n_q_tiles):
    pairs = [sorted({v, n_q_tiles - 1 - v}) for v in range((n_q_tiles + 1) // 2)]
    steps = [pairs[g] + pairs[len(pairs) - 1 - g] for g in range(len(pairs) // 2)]
    if len(pairs) % 2:
        steps.append(pairs[len(pairs) // 2])
    return steps


def _attn_kernel(q_ref, k_ref, vt_ref, bias_ref, lq_ref, g_ref, o_ref, s_sc, m_sc, acc_sc,
                 *, tile, n_q_tiles, lambda_init):
    n_chunks = 2 * tile // ATTN_QUERY_CHUNK
    chunk_cols = [slice(c * ATTN_QUERY_CHUNK, (c + 1) * ATTN_QUERY_CHUNK) for c in range(n_chunks)]

    def live_keys(c, diagonal):
        q_end = (c * ATTN_QUERY_CHUNK) % tile + ATTN_QUERY_CHUNK
        return q_end if diagonal else tile

    def produce(c, qi, j, bias, buf, diagonal):
        keys = live_keys(c, diagonal)
        q0 = qi * tile + (c * ATTN_QUERY_CHUNK) % tile
        qc = q_ref[(c * ATTN_QUERY_CHUNK) // tile, 0, q0:q0 + ATTN_QUERY_CHUNK, :]
        s = _dot_nt(k_ref[0, j * tile:j * tile + keys, :], qc)
        if bias is not None:
            b0 = (c * ATTN_QUERY_CHUNK) % tile
            s = s + bias[0:keys, b0:b0 + ATTN_QUERY_CHUNK]
        s_sc[buf, 0:keys, chunk_cols[c]] = s

    def consume(c, j, buf, m_ref, acc_ref, diagonal):
        keys = live_keys(c, diagonal)
        cols = chunk_cols[c]
        s = s_sc[buf, 0:keys, cols]
        m_prev = m_ref[:, cols]
        m_new = jnp.maximum(m_prev, jnp.max(s, axis=0, keepdims=True))
        alpha = jnp.exp2(m_prev - m_new)
        p = jnp.exp2(s - m_new).astype(BF16)
        acc_ref[:, cols] = alpha * acc_ref[:, cols] + _dot(vt_ref[0, 0, j, :, 0:keys], p)
        m_ref[:, cols] = m_new

    def finalize(qi, acc_ref):
        o = acc_ref[0:A_V_DIM] * (1.0 / acc_ref[A_V_DIM:A_V_DIM + 1])
        lq = lq_ref[...]
        lam = (jnp.exp(jnp.sum(lq[0:1] * lq[1:2], axis=-1, keepdims=True))
               - jnp.exp(jnp.sum(lq[2:3] * lq[3:4], axis=-1, keepdims=True)) + lambda_init)
        od = o[:, :tile] - lam * o[:, tile:]
        y = od * lax.rsqrt(jnp.mean(od * od, axis=0, keepdims=True) + EPS) * g_ref[...]
        o_ref[0, qi * tile:(qi + 1) * tile, :] = (y * (1.0 - lambda_init)).T.astype(BF16)

    def run(q_tiles):
        visits = []
        for slot, qi in enumerate(q_tiles):
            tiles = [(qi, bias_ref.at[0, 0], True)]
            if qi >= 1:
                tiles.append((qi - 1, bias_ref.at[0, 1], False))
            tiles += [(j, None, False) for j in range(qi - 1)]
            visits += [(qi, slot % 2, n == 0, n == len(tiles) - 1) + t for n, t in enumerate(tiles)]

        for n in range(-1, len(visits)):
            if n >= 0:
                qi, slot, first, last, j, _, diagonal = visits[n]
                m_ref, acc_ref = m_sc.at[slot], acc_sc.at[slot]
                if first:
                    m_ref[...] = jnp.full(m_ref.shape, -jnp.inf, F32)
                    acc_ref[...] = jnp.zeros(acc_ref.shape, F32)
            for c in range(n_chunks):
                if n + 1 < len(visits):
                    nqi, _, _, _, nj, nbias, ndiag = visits[n + 1]
                    produce(c, nqi, nj, nbias, (n + 1) % 2, ndiag)
                if n >= 0:
                    consume(c, j, n % 2, m_ref, acc_ref, diagonal)
            if n >= 0 and last:
                finalize(qi, acc_ref)

    for step, q_tiles in enumerate(_attn_step_tiles(n_q_tiles)):
        pl.when(pl.program_id(2) == step)(functools.partial(run, q_tiles))


def _diff_attention(q, k, vt, bias_tiles, lq, g_col, lambda_init):
    _, b, s, _ = q.shape
    tile = bias_tiles.shape[-1]
    tiles = s // tile
    hd = 2 * A_QK_DIM
    pipelined = (2 * s * hd * 2 + s * hd * 2 + s * A_VT_ROWS * 2 + 2 * tile * tile * 4
                 + s * A_V_DIM * 2)
    scratch = 2 * tile * 2 * tile * 4 + 2 * 8 * (2 * tile) * 4 + 2 * A_VT_ROWS * 2 * tile * 4
    return pl.pallas_call(
        functools.partial(_attn_kernel, tile=tile, n_q_tiles=tiles, lambda_init=lambda_init),
        out_shape=jax.ShapeDtypeStruct((b, s, A_WIDTH), BF16),
        grid=(b, A_HEADS, len(_attn_step_tiles(tiles))),
        in_specs=[
            pl.BlockSpec((2, 1, s, hd), lambda bi, h, i: (0, bi, 0, h)),
            pl.BlockSpec((1, s, hd), lambda bi, h, i: (bi, 0, h)),
            pl.BlockSpec((1, 1, tiles, A_VT_ROWS, tile), lambda bi, h, i: (bi, h, 0, 0, 0)),
            pl.BlockSpec((1, 2, tile, tile), lambda bi, h, i: (h, 0, 0, 0)),
            _resident(lq.shape),
            _resident(g_col.shape),
        ],
        out_specs=pl.BlockSpec((1, s, A_V_DIM), lambda bi, h, i: (bi, 0, h)),
        scratch_shapes=[
            pltpu.VMEM((2, tile, 2 * tile), F32),
            pltpu.VMEM((2, 1, 2 * tile), F32),
            pltpu.VMEM((2, A_VT_ROWS, 2 * tile), F32),
        ],
        compiler_params=_params(("parallel", "parallel", "arbitrary"),
                                _vmem_limit(pipelined, 128 * 128 * 4, scratch + 6 * tile * (2 * tile) * 4)),
        name="diff_attn",
    )(q, k, vt, bias_tiles, lq, g_col)


def _pool_fold_kernel(pw_ref, ps_ref, w_ref, o_ref):
    o_ref[0, 0:A_WIDTH, :] = w_ref[0, 0:A_WIDTH, :]
    for gi in range(len(POOL_WINDOWS)):
        cols = slice(gi * POOL_GROUP_DIM, (gi + 1) * POOL_GROUP_DIM)
        rows = slice(A_WIDTH + gi * POOL_GROUP_DIM, A_WIDTH + (gi + 1) * POOL_GROUP_DIM)
        pw = (pw_ref[0, gi] * ps_ref[0, :, cols]).astype(BF16)
        o_ref[0, rows, :] = _dot(pw, w_ref[0, rows, :]).astype(BF16)


def _pool_fold(pool_w, pool_scale, w_out):
    n, rows, d = w_out.shape
    return pl.pallas_call(
        _pool_fold_kernel,
        out_shape=jax.ShapeDtypeStruct(w_out.shape, BF16),
        grid=(n,),
        in_specs=[
            pl.BlockSpec((1,) + pool_w.shape[1:], lambda i: (i, 0, 0, 0)),
            pl.BlockSpec((1, 1, POOL_WIDTH), lambda i: (i, 0, 0)),
            pl.BlockSpec((1, rows, d), lambda i: (i, 0, 0)),
        ],
        out_specs=pl.BlockSpec((1, rows, d), lambda i: (i, 0, 0)),
        compiler_params=_params(("parallel",),
                                _vmem_limit(pool_w[0].size * 4 + POOL_WIDTH * 4 + 2 * rows * d * 2, 0,
                                            POOL_WIDTH * d * 4)),
        name="pool_fold",
    )(pool_w, pool_scale.reshape(n, 1, POOL_WIDTH), w_out)


def _even_out_kernel(h_ref, o_ref, u_ref, uh_ref, w_ref, out_ref, *, tm):
    i = pl.program_id(1)
    u = u_ref[0].astype(F32)
    halo = jnp.where(i > 0, uh_ref[0].astype(F32), 0.0)
    ext = jnp.concatenate([halo, u], axis=0)
    t = i * tm + lax.broadcasted_iota(jnp.int32, (tm, 1), 0)
    pooled = []
    for gi, w in enumerate(POOL_WINDOWS):
        cols = slice(gi * POOL_GROUP_DIM, (gi + 1) * POOL_GROUP_DIM)
        a = ext[:, cols]
        sh = 1
        while sh < w:
            a = a + pltpu.roll(a, sh, 0)
            sh *= 2
        inv_count = 1.0 / jnp.minimum(t + 1, w).astype(F32)
        pooled.append((a[POOL_HALO:] * inv_count - u[:, cols]).astype(BF16))
    mix = jnp.concatenate([o_ref[0]] + pooled, axis=-1)
    out_ref[0] = h_ref[0] + _dot(mix, w_ref[...])


def _even_out(h, o, u, w_mix):
    b, s, d = h.shape
    tm = min(ROW_TILE, s)
    halo_blocks = tm // POOL_HALO
    return pl.pallas_call(
        functools.partial(_even_out_kernel, tm=tm),
        out_shape=jax.ShapeDtypeStruct(h.shape, F32),
        grid=(b, s // tm),
        in_specs=[
            pl.BlockSpec((1, tm, d), lambda bi, i: (bi, i, 0)),
            pl.BlockSpec((1, tm, A_WIDTH), lambda bi, i: (bi, i, 0)),
            pl.BlockSpec((1, tm, POOL_WIDTH), lambda bi, i: (bi, i, 0)),
            pl.BlockSpec((1, POOL_HALO, POOL_WIDTH),
                         lambda bi, i: (bi, jnp.maximum(i * halo_blocks - 1, 0), 0)),
            _resident(w_mix.shape),
        ],
        out_specs=pl.BlockSpec((1, tm, d), lambda bi, i: (bi, i, 0)),
        compiler_params=_params(("parallel", "parallel"),
                                _vmem_limit(2 * tm * d * 4 + 2 * tm * 512 * 2 + POOL_HALO * 512 * 2,
                                            w_mix.size * 2, 4 * tm * d * 4)),
        name="even_out",
    )(h, o, u, u, w_mix)


def _odd_kernel(h_ref, g_ref, win_ref, cw_ref, wout_ref, out_ref, zprev_sc, *, tm):
    i = pl.program_id(1)
    h = h_ref[0]
    d = h.shape[-1]
    hn = _rms(h, g_ref[...]).astype(BF16)
    b_gate = _dot(hn, win_ref[:, 0:d])
    z = _dot(hn, win_ref[:, d:2 * d]) * _dot(hn, win_ref[:, 2 * d:3 * d])

    @pl.when(i == 0)
    def _():
        zprev_sc[...] = jnp.zeros(zprev_sc.shape, F32)

    ext = jnp.concatenate([zprev_sc[...], z], axis=0)
    zprev_sc[...] = z[tm - CONV_HALO:]
    cw = cw_ref[...]
    y = z * cw[CONV_WIDTH - 1:CONV_WIDTH]
    for tap in range(CONV_WIDTH - 1):
        back = CONV_WIDTH - 1 - tap
        y = y + pltpu.roll(ext, back, 0)[CONV_HALO:] * cw[tap:tap + 1]
    out_ref[0] = h + _dot((b_gate * y).astype(BF16), wout_ref[...])


def _odd_mixer(h, g, w_in, conv_w, w_out):
    b, s, d = h.shape
    tm = min(ROW_TILE, s)
    return pl.pallas_call(
        functools.partial(_odd_kernel, tm=tm),
        out_shape=jax.ShapeDtypeStruct(h.shape, F32),
        grid=(b, s // tm),
        in_specs=[
            pl.BlockSpec((1, tm, d), lambda bi, i: (bi, i, 0)),
            _resident(g.shape),
            _resident(w_in.shape),
            _resident(conv_w.shape),
            _resident(w_out.shape),
        ],
        out_specs=pl.BlockSpec((1, tm, d), lambda bi, i: (bi, i, 0)),
        scratch_shapes=[pltpu.VMEM((CONV_HALO, d), F32)],
        compiler_params=_params(("arbitrary", "arbitrary"),
                                _vmem_limit(2 * tm * d * 4, (w_in.size + w_out.size) * 2,
                                            8 * tm * d * 4)),
        name="odd_mixer",
    )(h, g, w_in, conv_w, w_out)


def _xattn_kernel(h_ref, g_ref, sw_ref, vw_ref, out_ref):
    h = h_ref[0]
    hn = _rms(h, g_ref[...]).astype(BF16)
    s = _dot(hn, sw_ref[0, 0])
    n_mem = s.shape[-1] // X_HEADS
    probs = []
    for head in range(X_HEADS):
        sh = s[:, head * n_mem:(head + 1) * n_mem]
        p = jnp.exp2(sh - jnp.max(sh, axis=-1, keepdims=True))
        probs.append((p * (1.0 / jnp.sum(p, axis=-1, keepdims=True))).astype(BF16))
    out_ref[0] = h + _dot(jnp.concatenate(probs, axis=-1), vw_ref[0, 0])


def _xattn(h, g, score_w, value_w, layer):
    b, s, d = h.shape
    n = score_w.shape[-1]
    tm = min(ROW_TILE, s)
    return pl.pallas_call(
        _xattn_kernel,
        out_shape=jax.ShapeDtypeStruct(h.shape, F32),
        grid=(b, s // tm),
        in_specs=[
            pl.BlockSpec((1, tm, d), lambda bi, i: (bi, i, 0)),
            _resident(g.shape),
            pl.BlockSpec((1, 1, d, n), lambda bi, i: (layer, bi, 0, 0)),
            pl.BlockSpec((1, 1, n, d), lambda bi, i: (layer, bi, 0, 0)),
        ],
        out_specs=pl.BlockSpec((1, tm, d), lambda bi, i: (bi, i, 0)),
        compiler_params=_params(("parallel", "parallel"),
                                _vmem_limit(2 * tm * d * 4 + 2 * d * n * 2, d * 4,
                                            2 * tm * d * 4 + 3 * tm * n * 4)),
        name="xattn",
    )(h, g, score_w, value_w)


def _mlp_kernel(h_ref, g_ref, w1_ref, w2_ref, gf_ref, out_ref, *, chunk, final_norm):
    h = h_ref[...]
    hn = _rms(h, g_ref[...]).astype(BF16)
    acc = h
    for c in range(w1_ref.shape[1] // chunk):
        a = jnp.maximum(_dot(hn, w1_ref[:, c * chunk:(c + 1) * chunk]), 0.0)
        acc = acc + _dot((a * a).astype(BF16), w2_ref[c * chunk:(c + 1) * chunk, :])
    out_ref[...] = _rms(acc, gf_ref[...]) if final_norm else acc


def _mlp(h2d, g, w1, w2, gf, final_norm):
    rows, d = h2d.shape
    tm = min(ROW_TILE, rows)
    chunk = min(1024, w1.shape[1])
    return pl.pallas_call(
        functools.partial(_mlp_kernel, chunk=chunk, final_norm=final_norm),
        out_shape=jax.ShapeDtypeStruct(h2d.shape, F32),
        grid=(rows // tm,),
        in_specs=[
            pl.BlockSpec((tm, d), lambda i: (i, 0)),
            _resident(g.shape),
            _resident(w1.shape),
            _resident(w2.shape),
            _resident(gf.shape),
        ],
        out_specs=pl.BlockSpec((tm, d), lambda i: (i, 0)),
        compiler_params=_params(("parallel",),
                                _vmem_limit(2 * tm * d * 4, (w1.size + w2.size) * 2,
                                            2 * tm * d * 4 + 3 * tm * chunk * 4)),
        name="mlp",
    )(h2d, g, w1, w2, gf)


def _t5_bucket(n):
    small = n < REL_MAX_EXACT
    nf = jnp.maximum(n, 1).astype(F32)
    large = REL_MAX_EXACT + (jnp.log(nf / REL_MAX_EXACT) / math.log(REL_MAX_DIST / REL_MAX_EXACT)
                             * (REL_BUCKETS - REL_MAX_EXACT)).astype(jnp.int32)
    return jnp.where(small, n, jnp.minimum(large, REL_BUCKETS - 1))


def _bias_tiles(rel_bias, tile):
    assert tile + 1 >= REL_MAX_DIST
    rb = rel_bias.astype(F32)
    rb = (rb - rb[REL_BUCKETS - 1]).T[:, :, None]
    heads = rb.shape[0]
    bucket = _t5_bucket(jnp.arange(2 * tile, dtype=jnp.int32))
    by_dist = jnp.zeros((heads, 2 * tile), F32)
    for bkt in range(REL_BUCKETS - 1):
        by_dist = jnp.where(bucket == bkt, rb[:, bkt], by_dist)
    by_dist = by_dist * LOG2E
    masked = jnp.full((heads, tile), NEG_BIG, F32)

    def toeplitz(vec):
        period = jnp.pad(vec, ((0, 0), (0, 1)))
        flat = jnp.tile(period, (1, tile))[:, :tile * 2 * tile]
        return flat.reshape(heads, tile, 2 * tile)[:, :, tile:]

    diag = toeplitz(jnp.concatenate([masked, by_dist[:, :tile]], axis=1))
    prev = toeplitz(by_dist)
    return jnp.stack([diag, prev], axis=1)


def kernel(x, mem, rel_bias, mem_norm_g, norm_mix_g, norm_xattn_g, norm_mlp_g, final_norm_g,
           ab_w_in, ab_w_out, lambda_q1, lambda_k1, lambda_q2, lambda_k2, subln_g, pool_w,
           pool_scale, conv_w_in, conv_w, conv_w_out, xattn_wq, xattn_wkv, xattn_wo, mlp_w1, mlp_w2):
    b, s, d = x.shape
    depth = norm_mix_g.shape[0]
    rows = b * s
    tile = min(ATTN_TILE, s)

    ab_w_in, ab_w_out, conv_w_in, conv_w_out, xattn_wq, xattn_wkv, xattn_wo, mlp_w1, mlp_w2 = (
        w.astype(BF16) for w in (ab_w_in, ab_w_out, conv_w_in, conv_w_out, xattn_wq,
                                 xattn_wkv, xattn_wo, mlp_w1, mlp_w2))

    bias_tiles = _bias_tiles(rel_bias, tile)
    score_w, value_w = _mem_fold(mem, mem_norm_g.reshape(1, d), xattn_wkv, xattn_wq, xattn_wo)
    ab_w_mix = _pool_fold(pool_w.astype(F32), pool_scale.astype(F32), ab_w_out)

    h = x
    for l in range(depth):
        i = l // 2
        g_mix = norm_mix_g[l].reshape(1, d)
        if l % 2 == 0:
            lambda_init = 0.8 - 0.6 * math.exp(-0.3 * l)
            q, k, vt, u = _even_in(h.reshape(rows, d), g_mix, ab_w_in[i], b, tile)
            lq = jnp.stack([lambda_q1[i], lambda_k1[i], lambda_q2[i], lambda_k2[i]]).astype(F32)
            o = _diff_attention(q.reshape(2, b, s, -1), k.reshape(b, s, -1), vt,
                                bias_tiles, lq, subln_g[i].reshape(-1, 1), lambda_init)
            h = _even_out(h, o, u.reshape(b, s, -1), ab_w_mix[i])
        else:
            h = _odd_mixer(h, g_mix, conv_w_in[i], conv_w[i], conv_w_out[i])
        h = _xattn(h, norm_xattn_g[l].reshape(1, d), score_w, value_w, l)
        h = _mlp(h.reshape(rows, d), norm_mlp_g[l].reshape(1, d), mlp_w1[l], mlp_w2[l],
                 final_norm_g.reshape(1, d), l == depth - 1).reshape(b, s, d)
    return h
```

```python
import functools
import math

import jax
import jax.numpy as jnp
from jax import lax
from jax.experimental import pallas as pl
from jax.experimental.pallas import tpu as pltpu

F32 = jnp.float32
BF16 = jnp.bfloat16

EPS = 1e-6
A_HEADS = 4
A_QK_DIM = 64
A_V_DIM = 2 * A_QK_DIM
A_WIDTH = A_HEADS * A_V_DIM
A_VT_ROWS = A_V_DIM + 16
POOL_WINDOWS = (2, 4, 8, 16)
POOL_GROUP_DIM = 128
POOL_WIDTH = len(POOL_WINDOWS) * POOL_GROUP_DIM
POOL_HALO = 16
CONV_WIDTH = 3
CONV_HALO = 8
X_HEADS = 4
REL_BUCKETS = 32
REL_MAX_EXACT = REL_BUCKETS // 2
REL_MAX_DIST = 128

V7X_VMEM_LIMIT_CAP = 56 * 1024 * 1024

ROW_TILE = 1024
ATTN_TILE = 512
ATTN_QUERY_CHUNK = 256
LOG2E = math.log2(math.e)
NEG_BIG = -1e30


def _vmem_limit(pipelined_bytes, resident_bytes, live_bytes):
    return min(2 * pipelined_bytes + resident_bytes + live_bytes, V7X_VMEM_LIMIT_CAP)


def _params(semantics, vmem):
    return pltpu.CompilerParams(dimension_semantics=semantics, vmem_limit_bytes=vmem)


def _resident(shape):
    return pl.BlockSpec(shape, lambda *_: (0,) * len(shape), pipeline_mode=pl.Buffered(1))


def _rms(x, g):
    return x * lax.rsqrt(jnp.mean(x * x, axis=-1, keepdims=True) + EPS) * g


def _dot(a, b):
    return jnp.dot(a, b, preferred_element_type=F32)


def _dot_nt(a, b):
    return lax.dot_general(a, b, (((1,), (1,)), ((), ())), preferred_element_type=F32)


def _mem_fold_kernel(x_ref, g_ref, wkv_ref, wq_ref, wo_ref, sw_ref, vw_ref):
    d = x_ref.shape[-1]
    hd = d // X_HEADS
    xn = _rms(x_ref[0], g_ref[...]).astype(BF16)
    kv = _dot(xn, wkv_ref[0]).astype(BF16)
    n_mem = kv.shape[0]
    for head in range(X_HEADS):
        cols = slice(head * hd, (head + 1) * hd)
        keys = slice(head * n_mem, (head + 1) * n_mem)
        sw = _dot_nt(wq_ref[0, :, cols], kv[:, cols]) * (hd ** -0.5 * LOG2E)
        sw_ref[0, 0, :, keys] = sw.astype(BF16)
        vw_ref[0, 0, keys, :] = _dot(kv[:, d + head * hd:d + (head + 1) * hd], wo_ref[0, cols, :]).astype(BF16)


def _mem_fold(mem, g, wkv, wq, wo):
    b, n_mem, d = mem.shape
    depth = wkv.shape[0]
    n = X_HEADS * n_mem
    layer_spec = lambda shape: pl.BlockSpec((1,) + shape, lambda l, bi: (l, 0, 0))
    return pl.pallas_call(
        _mem_fold_kernel,
        out_shape=(jax.ShapeDtypeStruct((depth, b, d, n), BF16),
                   jax.ShapeDtypeStruct((depth, b, n, d), BF16)),
        grid=(depth, b),
        in_specs=[
            pl.BlockSpec((1, n_mem, d), lambda l, bi: (bi, 0, 0)),
            _resident((1, d)),
            layer_spec((d, 2 * d)),
            layer_spec((d, d)),
            layer_spec((d, d)),
        ],
        out_specs=(pl.BlockSpec((1, 1, d, n), lambda l, bi: (l, bi, 0, 0)),
                   pl.BlockSpec((1, 1, n, d), lambda l, bi: (l, bi, 0, 0))),
        compiler_params=_params(("parallel", "parallel"),
                                _vmem_limit(n_mem * d * 4 + 4 * d * d * 2 + 2 * d * n * 2, d * 4,
                                            n_mem * 2 * d * 6 + 2 * d * n * 4)),
        name="mem_fold",
    )(mem, g, wkv, wq, wo)


def _even_in_kernel(x_ref, g_ref, w_ref, q_ref, k_ref, vt_ref, u_ref):
    xn = _rms(x_ref[...], g_ref[...]).astype(BF16)
    w = A_HEADS * 2 * A_QK_DIM
    q = _dot(xn, w_ref[:, 0:w]) * (A_QK_DIM ** -0.5 * LOG2E)
    first = (lax.broadcasted_iota(jnp.int32, q.shape, 1) % (2 * A_QK_DIM)) < A_QK_DIM
    q_ref[0] = jnp.where(first, q, 0.0).astype(BF16)
    q_ref[1] = jnp.where(first, 0.0, q).astype(BF16)
    k_ref[...] = _dot(xn, w_ref[:, w:2 * w]).astype(BF16)
    vt = _dot(xn, w_ref[:, 2 * w:2 * w + A_WIDTH]).T.astype(BF16)
    tile = vt_ref.shape[-1]
    pad_shape = (A_HEADS, A_VT_ROWS - A_V_DIM, tile)
    ones_row = lax.broadcasted_iota(jnp.int32, pad_shape, 1) == 0
    for t in range(vt_ref.shape[2]):
        vt_ref[0, :, t, 0:A_V_DIM] = vt[:, t * tile:(t + 1) * tile].reshape(A_HEADS, A_V_DIM, tile)
        vt_ref[0, :, t, A_V_DIM:A_VT_ROWS] = jnp.where(ones_row, 1.0, 0.0).astype(BF16)
    u_ref[...] = _dot(xn, w_ref[:, 2 * w + A_WIDTH:]).astype(BF16)


def _even_in(h2d, g, w_in, batch, tile):
    rows, d = h2d.shape
    n = w_in.shape[1]
    c = n // 4
    seq = rows // batch
    tiles = seq // tile
    tm = min(ROW_TILE, seq)
    steps = seq // tm
    row_spec = pl.BlockSpec((tm, c), lambda i: (i, 0))
    return pl.pallas_call(
        _even_in_kernel,
        out_shape=(jax.ShapeDtypeStruct((2, rows, c), BF16),
                   jax.ShapeDtypeStruct((rows, c), BF16),
                   jax.ShapeDtypeStruct((batch, A_HEADS, tiles, A_VT_ROWS, tile), BF16),
                   jax.ShapeDtypeStruct((rows, c), BF16)),
        grid=(rows // tm,),
        in_specs=[pl.BlockSpec((tm, d), lambda i: (i, 0)), _resident((1, d)), _resident((d, n))],
        out_specs=(pl.BlockSpec((2, tm, c), lambda i: (0, i, 0)),
                   row_spec,
                   pl.BlockSpec((1, A_HEADS, tm // tile, A_VT_ROWS, tile),
                                lambda i: (i // steps, 0, i % steps, 0, 0)),
                   row_spec),
        compiler_params=_params(("parallel",),
                                _vmem_limit(tm * d * 4 + 5 * tm * c * 2, d * n * 2 + d * 4, 4 * tm * d * 4)),
        name="even_in",
    )(h2d, g, w_in)


def _attn_step_tiles(n_q_tiles):
    pairs = [sorted({v, n_q_tiles - 1 - v}) for v in range((n_q_tiles + 1) // 2)]
    steps = [pairs[g] + pairs[len(pairs) - 1 - g] for g in range(len(pairs) // 2)]
    if len(pairs) % 2:
        steps.append(pairs[len(pairs) // 2])
    return steps


def _attn_kernel(q_ref, k_ref, vt_ref, bias_ref, lq_ref, g_ref, o_ref, s_sc, m_sc, acc_sc,
                 *, tile, n_q_tiles, lambda_init):
    n_chunks = 2 * tile // ATTN_QUERY_CHUNK
    chunk_cols = [slice(c * ATTN_QUERY_CHUNK, (c + 1) * ATTN_QUERY_CHUNK) for c in range(n_chunks)]

    def live_keys(c, diagonal):
        q_end = (c * ATTN_QUERY_CHUNK) % tile + ATTN_QUERY_CHUNK
        return q_end if diagonal else tile

    def produce(c, qi, j, bias, buf, diagonal):
        keys = live_keys(c, diagonal)
        q0 = qi * tile + (c * ATTN_QUERY_CHUNK) % tile
        qc = q_ref[(c * ATTN_QUERY_CHUNK) // tile, 0, q0:q0 + ATTN_QUERY_CHUNK, :]
        s = _dot_nt(k_ref[0, j * tile:j * tile + keys, :], qc)
        if bias is not None:
            b0 = (c * ATTN_QUERY_CHUNK) % tile
            s = s + bias[0:keys, b0:b0 + ATTN_QUERY_CHUNK]
        s_sc[buf, 0:keys, chunk_cols[c]] = s

    def consume(c, j, buf, m_ref, acc_ref, diagonal):
        keys = live_keys(c, diagonal)
        cols = chunk_cols[c]
        s = s_sc[buf, 0:keys, cols]
        m_prev = m_ref[:, cols]
        m_new = jnp.maximum(m_prev, jnp.max(s, axis=0, keepdims=True))
        alpha = jnp.exp2(m_prev - m_new)
        p = jnp.exp2(s - m_new).astype(BF16)
        acc_ref[:, cols] = alpha * acc_ref[:, cols] + _dot(vt_ref[0, 0, j, :, 0:keys], p)
        m_ref[:, cols] = m_new

    def finalize(qi, acc_ref):
        o = acc_ref[0:A_V_DIM] * (1.0 / acc_ref[A_V_DIM:A_V_DIM + 1])
        lq = lq_ref[...]
        lam = (jnp.exp(jnp.sum(lq[0:1] * lq[1:2], axis=-1, keepdims=True))
               - jnp.exp(jnp.sum(lq[2:3] * lq[3:4], axis=-1, keepdims=True)) + lambda_init)
        od = o[:, :tile] - lam * o[:, tile:]
        y = od * lax.rsqrt(jnp.mean(od * od, axis=0, keepdims=True) + EPS) * g_ref[...]
        o_ref[0, qi * tile:(qi + 1) * tile, :] = (y * (1.0 - lambda_init)).T.astype(BF16)

    def run(q_tiles):
        visits = []
        for slot, qi in enumerate(q_tiles):
            tiles = [(qi, bias_ref.at[0, 0], True)]
            if qi >= 1:
                tiles.append((qi - 1, bias_ref.at[0, 1], False))
            tiles += [(j, None, False) for j in range(qi - 1)]
            visits += [(qi, slot % 2, n == 0, n == len(tiles) - 1) + t for n, t in enumerate(tiles)]

        for n in range(-1, len(visits)):
            if n >= 0:
                qi, slot, first, last, j, _, diagonal = visits[n]
                m_ref, acc_ref = m_sc.at[slot], acc_sc.at[slot]
                if first:
                    m_ref[...] = jnp.full(m_ref.shape, -jnp.inf, F32)
                    acc_ref[...] = jnp.zeros(acc_ref.shape, F32)
            for c in range(n_chunks):
                if n + 1 < len(visits):
                    nqi, _, _, _, nj, nbias, ndiag = visits[n + 1]
                    produce(c, nqi, nj, nbias, (n + 1) % 2, ndiag)
                if n >= 0:
                    consume(c, j, n % 2, m_ref, acc_ref, diagonal)
            if n >= 0 and last:
                finalize(qi, acc_ref)

    for step, q_tiles in enumerate(_attn_step_tiles(n_q_tiles)):
        pl.when(pl.program_id(2) == step)(functools.partial(run, q_tiles))


def _diff_attention(q, k, vt, bias_tiles, lq, g_col, lambda_init):
    _, b, s, _ = q.shape
    tile = bias_tiles.shape[-1]
    tiles = s // tile
    hd = 2 * A_QK_DIM
    pipelined = (2 * s * hd * 2 + s * hd * 2 + s * A_VT_ROWS * 2 + 2 * tile * tile * 4
                 + s * A_V_DIM * 2)
    scratch = 2 * tile * 2 * tile * 4 + 2 * 8 * (2 * tile) * 4 + 2 * A_VT_ROWS * 2 * tile * 4
    return pl.pallas_call(
        functools.partial(_attn_kernel, tile=tile, n_q_tiles=tiles, lambda_init=lambda_init),
        out_shape=jax.ShapeDtypeStruct((b, s, A_WIDTH), BF16),
        grid=(b, A_HEADS, len(_attn_step_tiles(tiles))),
        in_specs=[
            pl.BlockSpec((2, 1, s, hd), lambda bi, h, i: (0, bi, 0, h)),
            pl.BlockSpec((1, s, hd), lambda bi, h, i: (bi, 0, h)),
            pl.BlockSpec((1, 1, tiles, A_VT_ROWS, tile), lambda bi, h, i: (bi, h, 0, 0, 0)),
            pl.BlockSpec((1, 2, tile, tile), lambda bi, h, i: (h, 0, 0, 0)),
            _resident(lq.shape),
            _resident(g_col.shape),
        ],
        out_specs=pl.BlockSpec((1, s, A_V_DIM), lambda bi, h, i: (bi, 0, h)),
        scratch_shapes=[
            pltpu.VMEM((2, tile, 2 * tile), F32),
            pltpu.VMEM((2, 1, 2 * tile), F32),
            pltpu.VMEM((2, A_VT_ROWS, 2 * tile), F32),
        ],
        compiler_params=_params(("parallel", "parallel", "arbitrary"),
                                _vmem_limit(pipelined, 128 * 128 * 4, scratch + 6 * tile * (2 * tile) * 4)),
        name="diff_attn",
    )(q, k, vt, bias_tiles, lq, g_col)


def _pool_fold_kernel(pw_ref, ps_ref, w_ref, o_ref):
    o_ref[0, 0:A_WIDTH, :] = w_ref[0, 0:A_WIDTH, :]
    for gi in range(len(POOL_WINDOWS)):
        cols = slice(gi * POOL_GROUP_DIM, (gi + 1) * POOL_GROUP_DIM)
        rows = slice(A_WIDTH + gi * POOL_GROUP_DIM, A_WIDTH + (gi + 1) * POOL_GROUP_DIM)
        pw = (pw_ref[0, gi] * ps_ref[0, :, cols]).astype(BF16)
        o_ref[0, rows, :] = _dot(pw, w_ref[0, rows, :]).astype(BF16)


def _pool_fold(pool_w, pool_scale, w_out):
    n, rows, d = w_out.shape
    return pl.pallas_call(
        _pool_fold_kernel,
        out_shape=jax.ShapeDtypeStruct(w_out.shape, BF16),
        grid=(n,),
        in_specs=[
            pl.BlockSpec((1,) + pool_w.shape[1:], lambda i: (i, 0, 0, 0)),
            pl.BlockSpec((1, 1, POOL_WIDTH), lambda i: (i, 0, 0)),
            pl.BlockSpec((1, rows, d), lambda i: (i, 0, 0)),
        ],
        out_specs=pl.BlockSpec((1, rows, d), lambda i: (i, 0, 0)),
        compiler_params=_params(("parallel",),
                                _vmem_limit(pool_w[0].size * 4 + POOL_WIDTH * 4 + 2 * rows * d * 2, 0,
                                            POOL_WIDTH * d * 4)),
        name="pool_fold",
    )(pool_w, pool_scale.reshape(n, 1, POOL_WIDTH), w_out)


def _even_tail_kernel(h_ref, o_ref, u_ref, uh_ref, w_ref, g_ref, sw_ref, vw_ref, out_ref, *, tm):
    i = pl.program_id(1)
    u = u_ref[0].astype(F32)
    halo = jnp.where(i > 0, uh_ref[0].astype(F32), 0.0)
    ext = jnp.concatenate([halo, u], axis=0)
    t = i * tm + lax.broadcasted_iota(jnp.int32, (tm, 1), 0)
    pooled = []
    for gi, w in enumerate(POOL_WINDOWS):
        cols = slice(gi * POOL_GROUP_DIM, (gi + 1) * POOL_GROUP_DIM)
        a = ext[:, cols]
        sh = 1
        while sh < w:
            a = a + pltpu.roll(a, sh, 0)
            sh *= 2
        inv_count = 1.0 / jnp.minimum(t + 1, w).astype(F32)
        pooled.append((a[POOL_HALO:] * inv_count - u[:, cols]).astype(BF16))
    mix = jnp.concatenate([o_ref[0]] + pooled, axis=-1)
    h = h_ref[0] + _dot(mix, w_ref[...])
    out_ref[0] = _xattn_rows(h, g_ref[...], sw_ref[0, 0], vw_ref[0, 0])


def _even_tail(h, o, u, w_mix, g, score_w, value_w, layer):
    b, s, d = h.shape
    n = score_w.shape[-1]
    tm = min(ROW_TILE, s)
    halo_blocks = tm // POOL_HALO
    return pl.pallas_call(
        functools.partial(_even_tail_kernel, tm=tm),
        out_shape=jax.ShapeDtypeStruct(h.shape, F32),
        grid=(b, s // tm),
        in_specs=[
            pl.BlockSpec((1, tm, d), lambda bi, i: (bi, i, 0)),
            pl.BlockSpec((1, tm, A_WIDTH), lambda bi, i: (bi, i, 0)),
            pl.BlockSpec((1, tm, POOL_WIDTH), lambda bi, i: (bi, i, 0)),
            pl.BlockSpec((1, POOL_HALO, POOL_WIDTH),
                         lambda bi, i: (bi, jnp.maximum(i * halo_blocks - 1, 0), 0)),
            _resident(w_mix.shape),
            _resident(g.shape),
            pl.BlockSpec((1, 1, d, n), lambda bi, i: (layer, bi, 0, 0)),
            pl.BlockSpec((1, 1, n, d), lambda bi, i: (layer, bi, 0, 0)),
        ],
        out_specs=pl.BlockSpec((1, tm, d), lambda bi, i: (bi, i, 0)),
        compiler_params=_params(("parallel", "parallel"),
                                _vmem_limit(2 * tm * d * 4 + 2 * tm * 512 * 2 + POOL_HALO * 512 * 2
                                            + 2 * d * n * 2,
                                            w_mix.size * 2, 4 * tm * d * 4 + 3 * tm * n * 4)),
        name="even_tail",
    )(h, o, u, u, w_mix, g, score_w, value_w)


def _odd_kernel(h_ref, g_ref, win_ref, cw_ref, wout_ref, out_ref, zprev_sc, *, tm):
    i = pl.program_id(1)
    h = h_ref[0]
    d = h.shape[-1]
    hn = _rms(h, g_ref[...]).astype(BF16)
    b_gate = _dot(hn, win_ref[:, 0:d])
    z = _dot(hn, win_ref[:, d:2 * d]) * _dot(hn, win_ref[:, 2 * d:3 * d])

    @pl.when(i == 0)
    def _():
        zprev_sc[...] = jnp.zeros(zprev_sc.shape, F32)

    ext = jnp.concatenate([zprev_sc[...], z], axis=0)
    zprev_sc[...] = z[tm - CONV_HALO:]
    cw = cw_ref[...]
    y = z * cw[CONV_WIDTH - 1:CONV_WIDTH]
    for tap in range(CONV_WIDTH - 1):
        back = CONV_WIDTH - 1 - tap
        y = y + pltpu.roll(ext, back, 0)[CONV_HALO:] * cw[tap:tap + 1]
    out_ref[0] = h + _dot((b_gate * y).astype(BF16), wout_ref[...])


def _odd_mixer(h, g, w_in, conv_w, w_out):
    b, s, d = h.shape
    tm = min(ROW_TILE, s)
    return pl.pallas_call(
        functools.partial(_odd_kernel, tm=tm),
        out_shape=jax.ShapeDtypeStruct(h.shape, F32),
        grid=(b, s // tm),
        in_specs=[
            pl.BlockSpec((1, tm, d), lambda bi, i: (bi, i, 0)),
            _resident(g.shape),
            _resident(w_in.shape),
            _resident(conv_w.shape),
            _resident(w_out.shape),
        ],
        out_specs=pl.BlockSpec((1, tm, d), lambda bi, i: (bi, i, 0)),
        scratch_shapes=[pltpu.VMEM((CONV_HALO, d), F32)],
        compiler_params=_params(("arbitrary", "arbitrary"),
                                _vmem_limit(2 * tm * d * 4, (w_in.size + w_out.size) * 2,
                                            8 * tm * d * 4)),
        name="odd_mixer",
    )(h, g, w_in, conv_w, w_out)


def _xattn_rows(h, g, score_w, value_w):
    hn = _rms(h, g).astype(BF16)
    s = _dot(hn, score_w)
    n_mem = s.shape[-1] // X_HEADS
    probs = []
    for head in range(X_HEADS):
        sh = s[:, head * n_mem:(head + 1) * n_mem]
        p = jnp.exp2(sh - jnp.max(sh, axis=-1, keepdims=True))
        probs.append((p * (1.0 / jnp.sum(p, axis=-1, keepdims=True))).astype(BF16))
    return h + _dot(jnp.concatenate(probs, axis=-1), value_w)


def _xattn_kernel(h_ref, g_ref, sw_ref, vw_ref, out_ref):
    out_ref[0] = _xattn_rows(h_ref[0], g_ref[...], sw_ref[0, 0], vw_ref[0, 0])


def _xattn(h, g, score_w, value_w, layer):
    b, s, d = h.shape
    n = score_w.shape[-1]
    tm = min(ROW_TILE, s)
    return pl.pallas_call(
        _xattn_kernel,
        out_shape=jax.ShapeDtypeStruct(h.shape, F32),
        grid=(b, s // tm),
        in_specs=[
            pl.BlockSpec((1, tm, d), lambda bi, i: (bi, i, 0)),
            _resident(g.shape),
            pl.BlockSpec((1, 1, d, n), lambda bi, i: (layer, bi, 0, 0)),
            pl.BlockSpec((1, 1, n, d), lambda bi, i: (layer, bi, 0, 0)),
        ],
        out_specs=pl.BlockSpec((1, tm, d), lambda bi, i: (bi, i, 0)),
        compiler_params=_params(("parallel", "parallel"),
                                _vmem_limit(2 * tm * d * 4 + 2 * d * n * 2, d * 4,
                                            2 * tm * d * 4 + 3 * tm * n * 4)),
        name="xattn",
    )(h, g, score_w, value_w)


def _mlp_kernel(h_ref, g_ref, w1_ref, w2_ref, gf_ref, out_ref, *, chunk, final_norm):
    h = h_ref[...]
    hn = _rms(h, g_ref[...]).astype(BF16)
    acc = h
    for c in range(w1_ref.shape[1] // chunk):
        a = jnp.maximum(_dot(hn, w1_ref[:, c * chunk:(c + 1) * chunk]), 0.0)
        acc = acc + _dot((a * a).astype(BF16), w2_ref[c * chunk:(c + 1) * chunk, :])
    out_ref[...] = _rms(acc, gf_ref[...]) if final_norm else acc


def _mlp(h2d, g, w1, w2, gf, final_norm):
    rows, d = h2d.shape
    tm = min(ROW_TILE, rows)
    chunk = min(1024, w1.shape[1])
    return pl.pallas_call(
        functools.partial(_mlp_kernel, chunk=chunk, final_norm=final_norm),
        out_shape=jax.ShapeDtypeStruct(h2d.shape, F32),
        grid=(rows // tm,),
        in_specs=[
            pl.BlockSpec((tm, d), lambda i: (i, 0)),
            _resident(g.shape),
            _resident(w1.shape),
            _resident(w2.shape),
            _resident(gf.shape),
        ],
        out_specs=pl.BlockSpec((tm, d), lambda i: (i, 0)),
        compiler_params=_params(("parallel",),
                                _vmem_limit(2 * tm * d * 4, (w1.size + w2.size) * 2,
                                            2 * tm * d * 4 + 3 * tm * chunk * 4)),
        name="mlp",
    )(h2d, g, w1, w2, gf)


def _t5_bucket(n):
    small = n < REL_MAX_EXACT
    nf = jnp.maximum(n, 1).astype(F32)
    large = REL_MAX_EXACT + (jnp.log(nf / REL_MAX_EXACT) / math.log(REL_MAX_DIST / REL_MAX_EXACT)
                             * (REL_BUCKETS - REL_MAX_EXACT)).astype(jnp.int32)
    return jnp.where(small, n, jnp.minimum(large, REL_BUCKETS - 1))


def _bias_tiles(rel_bias, tile):
    assert tile + 1 >= REL_MAX_DIST
    rb = rel_bias.astype(F32)
    rb = (rb - rb[REL_BUCKETS - 1]).T[:, :, None]
    heads = rb.shape[0]
    bucket = _t5_bucket(jnp.arange(2 * tile, dtype=jnp.int32))
    by_dist = jnp.zeros((heads, 2 * tile), F32)
    for bkt in range(REL_BUCKETS - 1):
        by_dist = jnp.where(bucket == bkt, rb[:, bkt], by_dist)
    by_dist = by_dist * LOG2E
    masked = jnp.full((heads, tile), NEG_BIG, F32)

    def toeplitz(vec):
        period = jnp.pad(vec, ((0, 0), (0, 1)))
        flat = jnp.tile(period, (1, tile))[:, :tile * 2 * tile]
        return flat.reshape(heads, tile, 2 * tile)[:, :, tile:]

    diag = toeplitz(jnp.concatenate([masked, by_dist[:, :tile]], axis=1))
    prev = toeplitz(by_dist)
    return jnp.stack([diag, prev], axis=1)


def kernel(x, mem, rel_bias, mem_norm_g, norm_mix_g, norm_xattn_g, norm_mlp_g, final_norm_g,
           ab_w_in, ab_w_out, lambda_q1, lambda_k1, lambda_q2, lambda_k2, subln_g, pool_w,
           pool_scale, conv_w_in, conv_w, conv_w_out, xattn_wq, xattn_wkv, xattn_wo, mlp_w1, mlp_w2):
    b, s, d = x.shape
    depth = norm_mix_g.shape[0]
    rows = b * s
    tile = min(ATTN_TILE, s)

    ab_w_in, ab_w_out, conv_w_in, conv_w_out, xattn_wq, xattn_wkv, xattn_wo, mlp_w1, mlp_w2 = (
        w.astype(BF16) for w in (ab_w_in, ab_w_out, conv_w_in, conv_w_out, xattn_wq,
                                 xattn_wkv, xattn_wo, mlp_w1, mlp_w2))

    bias_tiles = _bias_tiles(rel_bias, tile)
    score_w, value_w = _mem_fold(mem, mem_norm_g.reshape(1, d), xattn_wkv, xattn_wq, xattn_wo)
    ab_w_mix = _pool_fold(pool_w.astype(F32), pool_scale.astype(F32), ab_w_out)

    h = x
    for l in range(depth):
        i = l // 2
        g_mix = norm_mix_g[l].reshape(1, d)
        if l % 2 == 0:
            lambda_init = 0.8 - 0.6 * math.exp(-0.3 * l)
            q, k, vt, u = _even_in(h.reshape(rows, d), g_mix, ab_w_in[i], b, tile)
            lq = jnp.stack([lambda_q1[i], lambda_k1[i], lambda_q2[i], lambda_k2[i]]).astype(F32)
            o = _diff_attention(q.reshape(2, b, s, -1), k.reshape(b, s, -1), vt,
                                bias_tiles, lq, subln_g[i].reshape(-1, 1), lambda_init)
            h = _even_tail(h, o, u.reshape(b, s, -1), ab_w_mix[i], norm_xattn_g[l].reshape(1, d),
                           score_w, value_w, l)
        else:
            h = _odd_mixer(h, g_mix, conv_w_in[i], conv_w[i], conv_w_out[i])
            h = _xattn(h, norm_xattn_g[l].reshape(1, d), score_w, value_w, l)
        h = _mlp(h.reshape(rows, d), norm_mlp_g[l].reshape(1, d), mlp_w1[l], mlp_w2[l],
                 final_norm_g.reshape(1, d), l == depth - 1).reshape(b, s, d)
    return h
```

```python
import functools
import math

import jax
import jax.numpy as jnp
from jax import lax
from jax.experimental import pallas as pl
from jax.experimental.pallas import tpu as pltpu

F32 = jnp.float32
BF16 = jnp.bfloat16

EPS = 1e-6
A_HEADS = 4
A_QK_DIM = 64
A_V_DIM = 2 * A_QK_DIM
A_WIDTH = A_HEADS * A_V_DIM
A_VT_ROWS = A_V_DIM + 16
POOL_WINDOWS = (2, 4, 8, 16)
POOL_GROUP_DIM = 128
POOL_WIDTH = len(POOL_WINDOWS) * POOL_GROUP_DIM
POOL_HALO = 16
CONV_WIDTH = 3
CONV_HALO = 8
X_HEADS = 4
REL_BUCKETS = 32
REL_MAX_EXACT = REL_BUCKETS // 2
REL_MAX_DIST = 128

V7X_VMEM_LIMIT_CAP = 56 * 1024 * 1024
V7X_SUBLANES = 8
V7X_LANES = 128

ROW_TILE = 1024
ATTN_TILE = 512
ATTN_QUERY_CHUNK = 256
MLP_CHUNK = 1024
LOG2E = math.log2(math.e)
NEG_BIG = -1e30


def _vmem_limit(pipelined_bytes, resident_bytes, live_bytes):
    return min(2 * pipelined_bytes + resident_bytes + live_bytes, V7X_VMEM_LIMIT_CAP)


def _params(semantics, vmem):
    return pltpu.CompilerParams(dimension_semantics=semantics, vmem_limit_bytes=vmem)


def _resident(shape):
    return pl.BlockSpec(shape, lambda *_: (0,) * len(shape), pipeline_mode=pl.Buffered(1))


def _rms(x, g):
    return x * lax.rsqrt(jnp.mean(x * x, axis=-1, keepdims=True) + EPS) * g


def _dot(a, b):
    return jnp.dot(a, b, preferred_element_type=F32)


def _dot_nt(a, b):
    return lax.dot_general(a, b, (((1,), (1,)), ((), ())), preferred_element_type=F32)


def _mem_fold_kernel(x_ref, g_ref, wkv_ref, wq_ref, wo_ref, sw_ref, vw_ref):
    d = x_ref.shape[-1]
    hd = d // X_HEADS
    xn = _rms(x_ref[0], g_ref[...]).astype(BF16)
    kv = _dot(xn, wkv_ref[0]).astype(BF16)
    n_mem = kv.shape[0]
    for head in range(X_HEADS):
        cols = slice(head * hd, (head + 1) * hd)
        keys = slice(head * n_mem, (head + 1) * n_mem)
        sw = _dot_nt(wq_ref[0, :, cols], kv[:, cols]) * (hd ** -0.5 * LOG2E)
        sw_ref[0, 0, :, keys] = sw.astype(BF16)
        vw_ref[0, 0, keys, :] = _dot(kv[:, d + head * hd:d + (head + 1) * hd], wo_ref[0, cols, :]).astype(BF16)


def _mem_fold(mem, g, wkv, wq, wo):
    b, n_mem, d = mem.shape
    depth = wkv.shape[0]
    n = X_HEADS * n_mem
    layer_spec = lambda shape: pl.BlockSpec((1,) + shape, lambda l, bi: (l, 0, 0))
    return pl.pallas_call(
        _mem_fold_kernel,
        out_shape=(jax.ShapeDtypeStruct((depth, b, d, n), BF16),
                   jax.ShapeDtypeStruct((depth, b, n, d), BF16)),
        grid=(depth, b),
        in_specs=[
            pl.BlockSpec((1, n_mem, d), lambda l, bi: (bi, 0, 0)),
            _resident((1, d)),
            layer_spec((d, 2 * d)),
            layer_spec((d, d)),
            layer_spec((d, d)),
        ],
        out_specs=(pl.BlockSpec((1, 1, d, n), lambda l, bi: (l, bi, 0, 0)),
                   pl.BlockSpec((1, 1, n, d), lambda l, bi: (l, bi, 0, 0))),
        compiler_params=_params(("parallel", "parallel"),
                                _vmem_limit(n_mem * d * 4 + 4 * d * d * 2 + 2 * d * n * 2, d * 4,
                                            n_mem * 2 * d * 6 + 2 * d * n * 4)),
        name="mem_fold",
    )(mem, g, wkv, wq, wo)


def _even_in_kernel(x_ref, g_ref, w_ref, q_ref, k_ref, vt_ref, u_ref):
    xn = _rms(x_ref[...], g_ref[...]).astype(BF16)
    w = A_HEADS * 2 * A_QK_DIM
    q = _dot(xn, w_ref[:, 0:w]) * (A_QK_DIM ** -0.5 * LOG2E)
    first = (lax.broadcasted_iota(jnp.int32, q.shape, 1) % (2 * A_QK_DIM)) < A_QK_DIM
    q_ref[0] = jnp.where(first, q, 0.0).astype(BF16)
    q_ref[1] = jnp.where(first, 0.0, q).astype(BF16)
    k_ref[...] = _dot(xn, w_ref[:, w:2 * w]).astype(BF16)
    vt = _dot(xn, w_ref[:, 2 * w:2 * w + A_WIDTH]).T.astype(BF16)
    tile = vt_ref.shape[-1]
    pad_shape = (A_HEADS, A_VT_ROWS - A_V_DIM, tile)
    ones_row = lax.broadcasted_iota(jnp.int32, pad_shape, 1) == 0
    for t in range(vt_ref.shape[2]):
        vt_ref[0, :, t, 0:A_V_DIM] = vt[:, t * tile:(t + 1) * tile].reshape(A_HEADS, A_V_DIM, tile)
        vt_ref[0, :, t, A_V_DIM:A_VT_ROWS] = jnp.where(ones_row, 1.0, 0.0).astype(BF16)
    u_ref[...] = _dot(xn, w_ref[:, 2 * w + A_WIDTH:]).astype(BF16)


def _even_in(h2d, g, w_in, batch, tile):
    rows, d = h2d.shape
    n = w_in.shape[1]
    c = n // 4
    seq = rows // batch
    tiles = seq // tile
    tm = min(ROW_TILE, seq)
    steps = seq // tm
    row_spec = pl.BlockSpec((tm, c), lambda i: (i, 0))
    return pl.pallas_call(
        _even_in_kernel,
        out_shape=(jax.ShapeDtypeStruct((2, rows, c), BF16),
                   jax.ShapeDtypeStruct((rows, c), BF16),
                   jax.ShapeDtypeStruct((batch, A_HEADS, tiles, A_VT_ROWS, tile), BF16),
                   jax.ShapeDtypeStruct((rows, c), BF16)),
        grid=(rows // tm,),
        in_specs=[pl.BlockSpec((tm, d), lambda i: (i, 0)), _resident((1, d)), _resident((d, n))],
        out_specs=(pl.BlockSpec((2, tm, c), lambda i: (0, i, 0)),
                   row_spec,
                   pl.BlockSpec((1, A_HEADS, tm // tile, A_VT_ROWS, tile),
                                lambda i: (i // steps, 0, i % steps, 0, 0)),
                   row_spec),
        compiler_params=_params(("parallel",),
                                _vmem_limit(tm * d * 4 + 5 * tm * c * 2, d * n * 2 + d * 4, 4 * tm * d * 4)),
        name="even_in",
    )(h2d, g, w_in)


def _attn_step_tiles(n_q_tiles):
    pairs = [sorted({v, n_q_tiles - 1 - v}) for v in range((n_q_tiles + 1) // 2)]
    steps = [pairs[g] + pairs[len(pairs) - 1 - g] for g in range(len(pairs) // 2)]
    if len(pairs) % 2:
        steps.append(pairs[len(pairs) // 2])
    return steps


def _attn_kernel(q_ref, k_ref, vt_ref, bias_ref, lq_ref, g_ref, o_ref, s_sc, m_sc, acc_sc,
                 *, tile, n_q_tiles, lambda_init):
    n_chunks = 2 * tile // ATTN_QUERY_CHUNK
    chunk_cols = [slice(c * ATTN_QUERY_CHUNK, (c + 1) * ATTN_QUERY_CHUNK) for c in range(n_chunks)]

    def live_keys(c, diagonal):
        q_end = (c * ATTN_QUERY_CHUNK) % tile + ATTN_QUERY_CHUNK
        return q_end if diagonal else tile

    def produce(c, qi, j, bias, buf, diagonal):
        keys = live_keys(c, diagonal)
        q0 = qi * tile + (c * ATTN_QUERY_CHUNK) % tile
        qc = q_ref[(c * ATTN_QUERY_CHUNK) // tile, 0, q0:q0 + ATTN_QUERY_CHUNK, :]
        s = _dot_nt(k_ref[0, j * tile:j * tile + keys, :], qc)
        if bias is not None:
            b0 = (c * ATTN_QUERY_CHUNK) % tile
            s = s + bias[0:keys, b0:b0 + ATTN_QUERY_CHUNK]
        s_sc[buf, 0:keys, chunk_cols[c]] = s

    def consume(c, j, buf, m_ref, acc_ref, diagonal):
        keys = live_keys(c, diagonal)
        cols = chunk_cols[c]
        s = s_sc[buf, 0:keys, cols]
        m_prev = m_ref[:, cols]
        m_new = jnp.maximum(m_prev, jnp.max(s, axis=0, keepdims=True))
        alpha = jnp.exp2(m_prev - m_new)
        p = jnp.exp2(s - m_new).astype(BF16)
        acc_ref[:, cols] = alpha * acc_ref[:, cols] + _dot(vt_ref[0, 0, j, :, 0:keys], p)
        m_ref[:, cols] = m_new

    def finalize(qi, acc_ref):
        o = acc_ref[0:A_V_DIM] * (1.0 / acc_ref[A_V_DIM:A_V_DIM + 1])
        lq = lq_ref[...]
        lam = (jnp.exp(jnp.sum(lq[0:1] * lq[1:2], axis=-1, keepdims=True))
               - jnp.exp(jnp.sum(lq[2:3] * lq[3:4], axis=-1, keepdims=True)) + lambda_init)
        od = o[:, :tile] - lam * o[:, tile:]
        y = od * lax.rsqrt(jnp.mean(od * od, axis=0, keepdims=True) + EPS) * g_ref[...]
        o_ref[0, qi * tile:(qi + 1) * tile, :] = (y * (1.0 - lambda_init)).T.astype(BF16)

    def run(q_tiles):
        visits = []
        for slot, qi in enumerate(q_tiles):
            tiles = [(qi, bias_ref.at[0, 0], True)]
            if qi >= 1:
                tiles.append((qi - 1, bias_ref.at[0, 1], False))
            tiles += [(j, None, False) for j in range(qi - 1)]
            visits += [(qi, slot % 2, n == 0, n == len(tiles) - 1) + t for n, t in enumerate(tiles)]

        for n in range(-1, len(visits)):
            if n >= 0:
                qi, slot, first, last, j, _, diagonal = visits[n]
                m_ref, acc_ref = m_sc.at[slot], acc_sc.at[slot]
                if first:
                    m_ref[...] = jnp.full(m_ref.shape, -jnp.inf, F32)
                    acc_ref[...] = jnp.zeros(acc_ref.shape, F32)
            for c in range(n_chunks):
                if n + 1 < len(visits):
                    nqi, _, _, _, nj, nbias, ndiag = visits[n + 1]
                    produce(c, nqi, nj, nbias, (n + 1) % 2, ndiag)
                if n >= 0:
                    consume(c, j, n % 2, m_ref, acc_ref, diagonal)
            if n >= 0 and last:
                finalize(qi, acc_ref)

    for step, q_tiles in enumerate(_attn_step_tiles(n_q_tiles)):
        pl.when(pl.program_id(2) == step)(functools.partial(run, q_tiles))


def _diff_attention(q, k, vt, bias_tiles, lq, g_col, lambda_init):
    _, b, s, _ = q.shape
    tile = bias_tiles.shape[-1]
    tiles = s // tile
    hd = 2 * A_QK_DIM
    pipelined = (2 * s * hd * 2 + s * hd * 2 + s * A_VT_ROWS * 2 + 2 * tile * tile * 4
                 + s * A_V_DIM * 2)
    scratch = 2 * tile * 2 * tile * 4 + 2 * V7X_SUBLANES * (2 * tile) * 4 + 2 * A_VT_ROWS * 2 * tile * 4
    return pl.pallas_call(
        functools.partial(_attn_kernel, tile=tile, n_q_tiles=tiles, lambda_init=lambda_init),
        out_shape=jax.ShapeDtypeStruct((b, s, A_WIDTH), BF16),
        grid=(b, A_HEADS, len(_attn_step_tiles(tiles))),
        in_specs=[
            pl.BlockSpec((2, 1, s, hd), lambda bi, h, i: (0, bi, 0, h)),
            pl.BlockSpec((1, s, hd), lambda bi, h, i: (bi, 0, h)),
            pl.BlockSpec((1, 1, tiles, A_VT_ROWS, tile), lambda bi, h, i: (bi, h, 0, 0, 0)),
            pl.BlockSpec((1, 2, tile, tile), lambda bi, h, i: (h, 0, 0, 0)),
            _resident(lq.shape),
            _resident(g_col.shape),
        ],
        out_specs=pl.BlockSpec((1, s, A_V_DIM), lambda bi, h, i: (bi, 0, h)),
        scratch_shapes=[
            pltpu.VMEM((2, tile, 2 * tile), F32),
            pltpu.VMEM((2, 1, 2 * tile), F32),
            pltpu.VMEM((2, A_VT_ROWS, 2 * tile), F32),
        ],
        compiler_params=_params(("parallel", "parallel", "arbitrary"),
                                _vmem_limit(pipelined, A_V_DIM * V7X_LANES * 4, scratch + 6 * tile * (2 * tile) * 4)),
        name="diff_attn",
    )(q, k, vt, bias_tiles, lq, g_col)


def _pool_fold_kernel(pw_ref, ps_ref, w_ref, o_ref):
    o_ref[0, 0:A_WIDTH, :] = w_ref[0, 0:A_WIDTH, :]
    for gi in range(len(POOL_WINDOWS)):
        cols = slice(gi * POOL_GROUP_DIM, (gi + 1) * POOL_GROUP_DIM)
        rows = slice(A_WIDTH + gi * POOL_GROUP_DIM, A_WIDTH + (gi + 1) * POOL_GROUP_DIM)
        pw = (pw_ref[0, gi] * ps_ref[0, :, cols]).astype(BF16)
        o_ref[0, rows, :] = _dot(pw, w_ref[0, rows, :]).astype(BF16)


def _pool_fold(pool_w, pool_scale, w_out):
    n, rows, d = w_out.shape
    return pl.pallas_call(
        _pool_fold_kernel,
        out_shape=jax.ShapeDtypeStruct(w_out.shape, BF16),
        grid=(n,),
        in_specs=[
            pl.BlockSpec((1,) + pool_w.shape[1:], lambda i: (i, 0, 0, 0)),
            pl.BlockSpec((1, 1, POOL_WIDTH), lambda i: (i, 0, 0)),
            pl.BlockSpec((1, rows, d), lambda i: (i, 0, 0)),
        ],
        out_specs=pl.BlockSpec((1, rows, d), lambda i: (i, 0, 0)),
        compiler_params=_params(("parallel",),
                                _vmem_limit(pool_w[0].size * 4 + POOL_WIDTH * 4 + 2 * rows * d * 2, 0,
                                            POOL_WIDTH * d * 4)),
        name="pool_fold",
    )(pool_w, pool_scale.reshape(n, 1, POOL_WIDTH), w_out)


def _even_tail_kernel(h_ref, o_ref, u_ref, uh_ref, w_ref, g_ref, sw_ref, vw_ref, out_ref, *, tm):
    i = pl.program_id(1)
    u = u_ref[0].astype(F32)
    halo = jnp.where(i > 0, uh_ref[0].astype(F32), 0.0)
    ext = jnp.concatenate([halo, u], axis=0)
    t = i * tm + lax.broadcasted_iota(jnp.int32, (tm, 1), 0)
    pooled = []
    for gi, w in enumerate(POOL_WINDOWS):
        cols = slice(gi * POOL_GROUP_DIM, (gi + 1) * POOL_GROUP_DIM)
        a = ext[:, cols]
        sh = 1
        while sh < w:
            a = a + pltpu.roll(a, sh, 0)
            sh *= 2
        inv_count = 1.0 / jnp.minimum(t + 1, w).astype(F32)
        pooled.append((a[POOL_HALO:] * inv_count - u[:, cols]).astype(BF16))
    mix = jnp.concatenate([o_ref[0]] + pooled, axis=-1)
    h = h_ref[0] + _dot(mix, w_ref[...])
    out_ref[0] = _xattn_rows(h, g_ref[...], sw_ref[0, 0], vw_ref[0, 0])


def _even_tail(h, o, u, w_mix, g, score_w, value_w, layer):
    b, s, d = h.shape
    n = score_w.shape[-1]
    tm = min(ROW_TILE, s)
    halo_blocks = tm // POOL_HALO
    return pl.pallas_call(
        functools.partial(_even_tail_kernel, tm=tm),
        out_shape=jax.ShapeDtypeStruct(h.shape, F32),
        grid=(b, s // tm),
        in_specs=[
            pl.BlockSpec((1, tm, d), lambda bi, i: (bi, i, 0)),
            pl.BlockSpec((1, tm, A_WIDTH), lambda bi, i: (bi, i, 0)),
            pl.BlockSpec((1, tm, POOL_WIDTH), lambda bi, i: (bi, i, 0)),
            pl.BlockSpec((1, POOL_HALO, POOL_WIDTH),
                         lambda bi, i: (bi, jnp.maximum(i * halo_blocks - 1, 0), 0)),
            _resident(w_mix.shape),
            _resident(g.shape),
            pl.BlockSpec((1, 1, d, n), lambda bi, i: (layer, bi, 0, 0)),
            pl.BlockSpec((1, 1, n, d), lambda bi, i: (layer, bi, 0, 0)),
        ],
        out_specs=pl.BlockSpec((1, tm, d), lambda bi, i: (bi, i, 0)),
        compiler_params=_params(("parallel", "parallel"),
                                _vmem_limit(2 * tm * d * 4 + tm * (A_WIDTH + POOL_WIDTH) * 2 + POOL_HALO * POOL_WIDTH * 2
                                            + 2 * d * n * 2,
                                            w_mix.size * 2, 4 * tm * d * 4 + 3 * tm * n * 4)),
        name="even_tail",
    )(h, o, u, u, w_mix, g, score_w, value_w)


def _odd_kernel(h_ref, g_ref, win_ref, cw_ref, wout_ref, out_ref, zprev_sc, *, tm):
    i = pl.program_id(1)
    h = h_ref[0]
    d = h.shape[-1]
    hn = _rms(h, g_ref[...]).astype(BF16)
    b_gate = _dot(hn, win_ref[:, 0:d])
    z = _dot(hn, win_ref[:, d:2 * d]) * _dot(hn, win_ref[:, 2 * d:3 * d])

    @pl.when(i == 0)
    def _():
        zprev_sc[...] = jnp.zeros(zprev_sc.shape, F32)

    ext = jnp.concatenate([zprev_sc[...], z], axis=0)
    zprev_sc[...] = z[tm - CONV_HALO:]
    cw = cw_ref[...]
    y = z * cw[CONV_WIDTH - 1:CONV_WIDTH]
    for tap in range(CONV_WIDTH - 1):
        back = CONV_WIDTH - 1 - tap
        y = y + pltpu.roll(ext, back, 0)[CONV_HALO:] * cw[tap:tap + 1]
    out_ref[0] = h + _dot((b_gate * y).astype(BF16), wout_ref[...])


def _odd_mixer(h, g, w_in, conv_w, w_out):
    b, s, d = h.shape
    tm = min(ROW_TILE, s)
    return pl.pallas_call(
        functools.partial(_odd_kernel, tm=tm),
        out_shape=jax.ShapeDtypeStruct(h.shape, F32),
        grid=(b, s // tm),
        in_specs=[
            pl.BlockSpec((1, tm, d), lambda bi, i: (bi, i, 0)),
            _resident(g.shape),
            _resident(w_in.shape),
            _resident(conv_w.shape),
            _resident(w_out.shape),
        ],
        out_specs=pl.BlockSpec((1, tm, d), lambda bi, i: (bi, i, 0)),
        scratch_shapes=[pltpu.VMEM((CONV_HALO, d), F32)],
        compiler_params=_params(("arbitrary", "arbitrary"),
                                _vmem_limit(2 * tm * d * 4, (w_in.size + w_out.size) * 2,
                                            8 * tm * d * 4)),
        name="odd_mixer",
    )(h, g, w_in, conv_w, w_out)


def _xattn_rows(h, g, score_w, value_w):
    hn = _rms(h, g).astype(BF16)
    s = _dot(hn, score_w)
    n_mem = s.shape[-1] // X_HEADS
    probs = []
    for head in range(X_HEADS):
        sh = s[:, head * n_mem:(head + 1) * n_mem]
        p = jnp.exp2(sh - jnp.max(sh, axis=-1, keepdims=True))
        probs.append((p * (1.0 / jnp.sum(p, axis=-1, keepdims=True))).astype(BF16))
    return h + _dot(jnp.concatenate(probs, axis=-1), value_w)


def _xattn_kernel(h_ref, g_ref, sw_ref, vw_ref, out_ref):
    out_ref[0] = _xattn_rows(h_ref[0], g_ref[...], sw_ref[0, 0], vw_ref[0, 0])


def _xattn(h, g, score_w, value_w, layer):
    b, s, d = h.shape
    n = score_w.shape[-1]
    tm = min(ROW_TILE, s)
    return pl.pallas_call(
        _xattn_kernel,
        out_shape=jax.ShapeDtypeStruct(h.shape, F32),
        grid=(b, s // tm),
        in_specs=[
            pl.BlockSpec((1, tm, d), lambda bi, i: (bi, i, 0)),
            _resident(g.shape),
            pl.BlockSpec((1, 1, d, n), lambda bi, i: (layer, bi, 0, 0)),
            pl.BlockSpec((1, 1, n, d), lambda bi, i: (layer, bi, 0, 0)),
        ],
        out_specs=pl.BlockSpec((1, tm, d), lambda bi, i: (bi, i, 0)),
        compiler_params=_params(("parallel", "parallel"),
                                _vmem_limit(2 * tm * d * 4 + 2 * d * n * 2, d * 4,
                                            2 * tm * d * 4 + 3 * tm * n * 4)),
        name="xattn",
    )(h, g, score_w, value_w)


def _mlp_kernel(h_ref, g_ref, w1_ref, w2_ref, gf_ref, out_ref, *, chunk, final_norm):
    h = h_ref[...]
    hn = _rms(h, g_ref[...]).astype(BF16)
    acc = h
    for c in range(w1_ref.shape[1] // chunk):
        a = jnp.maximum(_dot(hn, w1_ref[:, c * chunk:(c + 1) * chunk]), 0.0)
        acc = acc + _dot((a * a).astype(BF16), w2_ref[c * chunk:(c + 1) * chunk, :])
    out_ref[...] = _rms(acc, gf_ref[...]) if final_norm else acc


def _mlp(h2d, g, w1, w2, gf, final_norm):
    rows, d = h2d.shape
    tm = min(ROW_TILE, rows)
    chunk = min(MLP_CHUNK, w1.shape[1])
    return pl.pallas_call(
        functools.partial(_mlp_kernel, chunk=chunk, final_norm=final_norm),
        out_shape=jax.ShapeDtypeStruct(h2d.shape, F32),
        grid=(rows // tm,),
        in_specs=[
            pl.BlockSpec((tm, d), lambda i: (i, 0)),
            _resident(g.shape),
            _resident(w1.shape),
            _resident(w2.shape),
            _resident(gf.shape),
        ],
        out_specs=pl.BlockSpec((tm, d), lambda i: (i, 0)),
        compiler_params=_params(("parallel",),
                                _vmem_limit(2 * tm * d * 4, (w1.size + w2.size) * 2,
                                            2 * tm * d * 4 + 3 * tm * chunk * 4)),
        name="mlp",
    )(h2d, g, w1, w2, gf)


def _t5_bucket(n):
    small = n < REL_MAX_EXACT
    nf = jnp.maximum(n, 1).astype(F32)
    large = REL_MAX_EXACT + (jnp.log(nf / REL_MAX_EXACT) / math.log(REL_MAX_DIST / REL_MAX_EXACT)
                             * (REL_BUCKETS - REL_MAX_EXACT)).astype(jnp.int32)
    return jnp.where(small, n, jnp.minimum(large, REL_BUCKETS - 1))


def _bias_tiles(rel_bias, tile):
    assert tile + 1 >= REL_MAX_DIST
    rb = rel_bias.astype(F32)
    rb = (rb - rb[REL_BUCKETS - 1]).T[:, :, None]
    heads = rb.shape[0]
    bucket = _t5_bucket(jnp.arange(2 * tile, dtype=jnp.int32))
    by_dist = jnp.zeros((heads, 2 * tile), F32)
    for bkt in range(REL_BUCKETS - 1):
        by_dist = jnp.where(bucket == bkt, rb[:, bkt], by_dist)
    by_dist = by_dist * LOG2E
    masked = jnp.full((heads, tile), NEG_BIG, F32)

    def toeplitz(vec):
        period = jnp.pad(vec, ((0, 0), (0, 1)))
        flat = jnp.tile(period, (1, tile))[:, :tile * 2 * tile]
        return flat.reshape(heads, tile, 2 * tile)[:, :, tile:]

    diag = toeplitz(jnp.concatenate([masked, by_dist[:, :tile]], axis=1))
    prev = toeplitz(by_dist)
    return jnp.stack([diag, prev], axis=1)


def kernel(x, mem, rel_bias, mem_norm_g, norm_mix_g, norm_xattn_g, norm_mlp_g, final_norm_g,
           ab_w_in, ab_w_out, lambda_q1, lambda_k1, lambda_q2, lambda_k2, subln_g, pool_w,
           pool_scale, conv_w_in, conv_w, conv_w_out, xattn_wq, xattn_wkv, xattn_wo, mlp_w1, mlp_w2):
    b, s, d = x.shape
    depth = norm_mix_g.shape[0]
    rows = b * s
    tile = min(ATTN_TILE, s)

    ab_w_in, ab_w_out, conv_w_in, conv_w_out, xattn_wq, xattn_wkv, xattn_wo, mlp_w1, mlp_w2 = (
        w.astype(BF16) for w in (ab_w_in, ab_w_out, conv_w_in, conv_w_out, xattn_wq,
                                 xattn_wkv, xattn_wo, mlp_w1, mlp_w2))

    bias_tiles = _bias_tiles(rel_bias, tile)
    score_w, value_w = _mem_fold(mem, mem_norm_g.reshape(1, d), xattn_wkv, xattn_wq, xattn_wo)
    ab_w_mix = _pool_fold(pool_w.astype(F32), pool_scale.astype(F32), ab_w_out)

    h = x
    for l in range(depth):
        i = l // 2
        g_mix = norm_mix_g[l].reshape(1, d)
        if l % 2 == 0:
            lambda_init = 0.8 - 0.6 * math.exp(-0.3 * l)
            q, k, vt, u = _even_in(h.reshape(rows, d), g_mix, ab_w_in[i], b, tile)
            lq = jnp.stack([lambda_q1[i], lambda_k1[i], lambda_q2[i], lambda_k2[i]]).astype(F32)
            o = _diff_attention(q.reshape(2, b, s, -1), k.reshape(b, s, -1), vt,
                                bias_tiles, lq, subln_g[i].reshape(-1, 1), lambda_init)
            h = _even_tail(h, o, u.reshape(b, s, -1), ab_w_mix[i], norm_xattn_g[l].reshape(1, d),
                           score_w, value_w, l)
        else:
            h = _odd_mixer(h, g_mix, conv_w_in[i], conv_w[i], conv_w_out[i])
            h = _xattn(h, norm_xattn_g[l].reshape(1, d), score_w, value_w, l)
        h = _mlp(h.reshape(rows, d), norm_mlp_g[l].reshape(1, d), mlp_w1[l], mlp_w2[l],
                 final_norm_g.reshape(1, d), l == depth - 1).reshape(b, s, d)
    return h
```
